```python
import math
import jax, jax.numpy as jnp
from jax import lax
import numpy as np

D_MODEL = 2048
BATCH = 16
SEQ = 2048
DEPTH = 1
DEC_BATCH = 8
DEC_SEQ = 64
PAST_LEN = 2048

CHUNK = 64
MIX_W = D_MODEL
ATT_W = MIX_W // 2
LRU_W = MIX_W - ATT_W
N_HEADS = 8
V_DIM = ATT_W // N_HEADS
QK_DIM = V_DIM // 2
ROT_DIM = QK_DIM // 4
ROPE_THETA = 500000.0
Q_BLOCK = 128
N_LRU_BLOCKS = 8
LRU_BLOCK = LRU_W // N_LRU_BLOCKS
CONV_W = 4
LRU_C = 8.0
D_FF = 4 * D_MODEL
N_MOD = 6
IN_COLS = 3 * ATT_W + 2 * LRU_W
EPS = 1e-6

kernel_name = "chunk_causal_hymba_diffattn_rglru_step"


def rms_norm(x, eps=EPS):
    xf = x.astype(jnp.float32)
    return (xf * lax.rsqrt(jnp.mean(xf * xf, axis=-1, keepdims=True) + eps)).astype(x.dtype)


def lambda_init_fn(layer):
    return 0.8 - 0.6 * math.exp(-0.3 * layer)


def partial_rope(x, pos):
    half = ROT_DIM // 2
    inv_freq = ROPE_THETA ** (-(jnp.arange(half, dtype=jnp.float32) * 2.0) / ROT_DIM)
    ang = pos.astype(jnp.float32)[:, None] * inv_freq[None, :]
    cos = jnp.cos(ang)[:, None, None, :]
    sin = jnp.sin(ang)[:, None, None, :]
    xf = x.astype(jnp.float32)
    x1 = xf[..., :half]
    x2 = xf[..., half:ROT_DIM]
    out = jnp.concatenate([x1 * cos - x2 * sin, x2 * cos + x1 * sin, xf[..., ROT_DIM:]], axis=-1)
    return out.astype(x.dtype)


def modulation(c, w_ada, b_ada):
    mod = jax.nn.silu(c) @ w_ada + b_ada
    return jnp.split(mod[:, None, :], N_MOD, axis=-1)


def diff_attend(q, k, v, q_pos, k_pos, lam, g_subln, lam_init):
    s = jnp.einsum('bqhcd,bkhcd->bhcqk', q, k, preferred_element_type=jnp.float32) * (QK_DIM ** -0.5)
    visible = (k_pos[None, :] // CHUNK) <= (q_pos[:, None] // CHUNK)
    s = jnp.where(visible, s, -1e30)
    p = jax.nn.softmax(s, axis=-1)
    w = (p[:, :, 0] - lam * p[:, :, 1]).astype(v.dtype)
    o = jnp.einsum('bhqk,bkhd->bqhd', w, v)
    return rms_norm(o) * g_subln * (1.0 - lam_init)


def _lin_combine(left, right):
    a_l, b_l = left
    a_r, b_r = right
    return a_l * a_r, a_r * b_l + b_r


def rglru_branch(xb, gb, h0, conv0, conv_w, conv_b, w_a, b_a, w_x, b_x, lru_lambda):
    B, T, W = xb.shape
    buf = jnp.concatenate([conv0.astype(xb.dtype), xb], axis=1)
    u = conv_b + buf[:, 0:T] * conv_w[0]
    for j in range(1, CONV_W):
        u = u + buf[:, j:j + T] * conv_w[j]
    new_conv = buf[:, -(CONV_W - 1):]
    ub = u.reshape(B, T, N_LRU_BLOCKS, LRU_BLOCK)
    r = jax.nn.sigmoid(jnp.einsum('btnd,nde->btne', ub, w_a) + b_a).reshape(B, T, W)
    i = jax.nn.sigmoid(jnp.einsum('btnd,nde->btne', ub, w_x) + b_x).reshape(B, T, W)
    log_a = -LRU_C * r.astype(jnp.float32) * jax.nn.softplus(-lru_lambda.astype(jnp.float32))
    a = jnp.exp(log_a)
    b = jnp.sqrt(-jnp.expm1(2.0 * log_a)) * (i * u).astype(jnp.float32)
    b = b.at[:, 0].add(a[:, 0] * h0.astype(jnp.float32))
    _, h = lax.associative_scan(_lin_combine, (a, b), axis=1)
    y = h.astype(xb.dtype) * jax.nn.gelu(gb, approximate=True)
    return y, h[:, -1].astype(xb.dtype), new_conv


def hybrid_layer(x, c, pos, past_k, past_v, h0, conv0,
                 w_ada, b_ada, w_in, g_q, g_k, lambda_q1, lambda_k1, lambda_q2, lambda_k2, g_subln,
                 conv_w, conv_b, w_gate_a, b_gate_a, w_gate_x, b_gate_x, lru_lambda,
                 w_out, w_up, w_down, lam_init):
    B, T, _ = x.shape
    sh1, sc1, gt1, sh2, sc2, gt2 = modulation(c, w_ada, b_ada)
    xn = rms_norm(x) * (1.0 + sc1) + sh1
    proj = xn @ w_in
    q, k, v, xb, gb = jnp.split(proj, [ATT_W, 2 * ATT_W, 3 * ATT_W, 3 * ATT_W + LRU_W], axis=-1)
    q = partial_rope(rms_norm(q.reshape(B, T, N_HEADS, 2, QK_DIM)) * g_q, pos)
    k = partial_rope(rms_norm(k.reshape(B, T, N_HEADS, 2, QK_DIM)) * g_k, pos)
    v = v.reshape(B, T, N_HEADS, V_DIM)
    lam = (jnp.exp(jnp.sum(lambda_q1 * lambda_k1).astype(jnp.float32))
           - jnp.exp(jnp.sum(lambda_q2 * lambda_k2).astype(jnp.float32)) + lam_init)
    if past_k is None:
        nb = T // Q_BLOCK
        qb = jnp.moveaxis(q.reshape(B, nb, Q_BLOCK, N_HEADS, 2, QK_DIM), 1, 0)
        pb = pos.reshape(nb, Q_BLOCK)
        ob = lax.map(lambda a: diff_attend(a[0], k, v, a[1], pos, lam, g_subln, lam_init), (qb, pb))
        o = jnp.moveaxis(ob, 0, 1).reshape(B, T, ATT_W)
    else:
        k_all = jnp.concatenate([past_k.astype(k.dtype), k], axis=1)
        v_all = jnp.concatenate([past_v.astype(v.dtype), v], axis=1)
        k_pos = jnp.arange(k_all.shape[1])
        o = diff_attend(q, k_all, v_all, pos, k_pos, lam, g_subln, lam_init).reshape(B, T, ATT_W)
    y_lru, h_new, conv_new = rglru_branch(xb, gb, h0, conv0, conv_w, conv_b,
                                          w_gate_a, b_gate_a, w_gate_x, b_gate_x, lru_lambda)
    mix = jnp.concatenate([o, y_lru], axis=-1) @ w_out
    x = x + gt1 * mix
    xn2 = rms_norm(x) * (1.0 + sc2) + sh2
    x = x + gt2 * (jnp.square(jax.nn.relu(xn2 @ w_up)) @ w_down)
    return x, k, v, h_new, conv_new


def setup_inputs(seed: int = 0) -> dict:
    key = jax.random.key(seed)
    ks = jax.random.split(key, 32)
    L = DEPTH

    def nrm(k, shape, s):
        return jax.random.normal(k, shape, jnp.float32) * s

    a0 = jax.random.uniform(ks[20], (L, LRU_W), jnp.float32, minval=0.9, maxval=0.999)
    sg = a0 ** (1.0 / LRU_C)
    lru_lambda = jnp.log(sg) - jnp.log1p(-sg)
    return {
        "x_prompt": nrm(ks[0], (BATCH, SEQ, D_MODEL), 1.0),
        "x_sample": nrm(ks[1], (DEC_BATCH, DEC_SEQ, D_MODEL), 1.0),
        "c_prompt": nrm(ks[2], (BATCH, D_MODEL), 1.0),
        "c_sample": nrm(ks[3], (DEC_BATCH, D_MODEL), 1.0),
        "cache_k": nrm(ks[4], (L, DEC_BATCH, PAST_LEN, N_HEADS, 2, QK_DIM), 1.0),
        "cache_v": nrm(ks[5], (L, DEC_BATCH, PAST_LEN, N_HEADS, V_DIM), 1.0),
        "state_lru_h": nrm(ks[6], (L, DEC_BATCH, LRU_W), 0.5),
        "state_conv": nrm(ks[7], (L, DEC_BATCH, CONV_W - 1, LRU_W), 1.0),
        "w_ada": nrm(ks[8], (L, D_MODEL, N_MOD * D_MODEL), 0.5 * D_MODEL ** -0.5),
        "b_ada": nrm(ks[9], (L, N_MOD * D_MODEL), 0.02),
        "w_in": nrm(ks[10], (L, D_MODEL, IN_COLS), D_MODEL ** -0.5),
        "g_q": 1.0 + nrm(ks[11], (L, QK_DIM), 0.02),
        "g_k": 1.0 + nrm(ks[12], (L, QK_DIM), 0.02),
        "lambda_q1": nrm(ks[13], (L, QK_DIM), 0.1),
        "lambda_k1": nrm(ks[14], (L, QK_DIM), 0.1),
        "lambda_q2": nrm(ks[15], (L, QK_DIM), 0.1),
        "lambda_k2": nrm(ks[16], (L, QK_DIM), 0.1),
        "g_subln": 1.0 + nrm(ks[17], (L, V_DIM), 0.02),
        "conv_w": nrm(ks[18], (L, CONV_W, LRU_W), CONV_W ** -0.5),
        "conv_b": nrm(ks[19], (L, LRU_W), 0.02),
        "w_gate_a": nrm(ks[21], (L, N_LRU_BLOCKS, LRU_BLOCK, LRU_BLOCK), LRU_BLOCK ** -0.5),
        "b_gate_a": nrm(ks[22], (L, N_LRU_BLOCKS, LRU_BLOCK), 0.02),
        "w_gate_x": nrm(ks[23], (L, N_LRU_BLOCKS, LRU_BLOCK, LRU_BLOCK), LRU_BLOCK ** -0.5),
        "b_gate_x": nrm(ks[24], (L, N_LRU_BLOCKS, LRU_BLOCK), 0.02),
        "lru_lambda": lru_lambda,
        "w_out": nrm(ks[25], (L, MIX_W, D_MODEL), MIX_W ** -0.5),
        "w_up": nrm(ks[26], (L, D_MODEL, D_FF), D_MODEL ** -0.5),
        "w_down": nrm(ks[27], (L, D_FF, D_MODEL), D_FF ** -0.5),
    }


def reference(x_prompt, x_sample, c_prompt, c_sample, cache_k, cache_v, state_lru_h, state_conv,
              w_ada, b_ada, w_in, g_q, g_k, lambda_q1, lambda_k1, lambda_q2, lambda_k2, g_subln,
              conv_w, conv_b, w_gate_a, b_gate_a, w_gate_x, b_gate_x, lru_lambda,
              w_out, w_up, w_down):
    B, T, _ = x_prompt.shape
    Bs, Ts, _ = x_sample.shape
    pos_p = jnp.arange(T)
    pos_s = cache_k.shape[2] + jnp.arange(Ts)
    xp, xs = x_prompt, x_sample
    kp_l, vp_l, hp_l, cp_l = [], [], [], []
    ks_l, vs_l, hs_l, cs_l = [], [], [], []
    for d in range(DEPTH):
        lw = (w_ada[d], b_ada[d], w_in[d], g_q[d], g_k[d], lambda_q1[d], lambda_k1[d],
              lambda_q2[d], lambda_k2[d], g_subln[d], conv_w[d], conv_b[d], w_gate_a[d], b_gate_a[d],
              w_gate_x[d], b_gate_x[d], lru_lambda[d], w_out[d], w_up[d], w_down[d])
        lam_init = lambda_init_fn(d)
        h0 = jnp.zeros((B, LRU_W), xp.dtype)
        c0 = jnp.zeros((B, CONV_W - 1, LRU_W), xp.dtype)
        xp, kp, vp, hp, cp = hybrid_layer(xp, c_prompt, pos_p, None, None, h0, c0, *lw, lam_init)
        xs, kn, vn, hn, cn = hybrid_layer(xs, c_sample, pos_s, cache_k[d], cache_v[d],
                                          state_lru_h[d], state_conv[d], *lw, lam_init)
        kp_l.append(kp); vp_l.append(vp); hp_l.append(hp); cp_l.append(cp)
        ks_l.append(kn); vs_l.append(vn); hs_l.append(hn); cs_l.append(cn)
    return (xp, xs,
            jnp.stack(kp_l), jnp.stack(vp_l), jnp.stack(hp_l), jnp.stack(cp_l),
            jnp.stack(ks_l), jnp.stack(vs_l), jnp.stack(hs_l), jnp.stack(cs_l))
```

```python
import functools
import math

import jax
import jax.numpy as jnp
from jax import lax
from jax.experimental import pallas as pl
from jax.experimental.pallas import tpu as pltpu

F32 = jnp.float32
BF16 = jnp.bfloat16

LANES = 128
SUBLANES = 8
VMEM_LIMIT = 56 * 1024 * 1024

N_HEADS = 8
QK_DIM = 64
V_DIM = 2 * QK_DIM
CHUNK = 64
ROT_DIM = QK_DIM // 4
ROPE_THETA = 500000.0
N_LRU_BLOCKS = 8
CONV_W = 4
LRU_C = 8.0
N_MOD = 6
EPS = 1e-6
LAM_INIT = 0.8 - 0.6 * math.exp(-0.3 * 0)


def _params(n_axes):
    return pltpu.CompilerParams(dimension_semantics=("arbitrary",) * n_axes,
                                vmem_limit_bytes=VMEM_LIMIT)


def _mod_body(c_ref, w_ref, b_ref, o_ref):
    c = c_ref[...]
    s = c * jax.nn.sigmoid(c)
    o_ref[...] = jnp.dot(s.astype(BF16), w_ref[...].astype(BF16),
                         preferred_element_type=F32) + b_ref[...]


def _modulation(c, w_ada, b_ada, tn=1024):
    rows, d = c.shape
    n = w_ada.shape[1]
    return pl.pallas_call(
        _mod_body,
        grid=(n // tn,),
        in_specs=[pl.BlockSpec((rows, d), lambda j: (0, 0)),
                  pl.BlockSpec((d, tn), lambda j: (0, j)),
                  pl.BlockSpec((1, tn), lambda j: (0, j))],
        out_specs=pl.BlockSpec((rows, tn), lambda j: (0, j)),
        out_shape=jax.ShapeDtypeStruct((rows, n), F32),
        compiler_params=_params(1),
        name="modulation",
    )(c, w_ada, b_ada.reshape(1, n))


def _mod_spec(bb, d, which, n_axes):
    if n_axes == 3:
        return pl.BlockSpec((bb, None, 1, d), lambda b, i, j: (b, which, 0, 0))
    return pl.BlockSpec((bb, None, 1, d), lambda b, i: (b, which, 0, 0))


def _rope_tables(pos):
    half = ROT_DIM // 2
    inv_freq = ROPE_THETA ** (-(jnp.arange(half, dtype=F32) * 2.0) / ROT_DIM)
    ang = pos.astype(F32)[:, None] * inv_freq[None, :]
    cos, sin = jnp.cos(ang), jnp.sin(ang)
    t = pos.shape[0]
    ones = jnp.ones((t, QK_DIM - ROT_DIM), F32)
    zeros_h = jnp.zeros((t, half), F32)
    zeros_r = jnp.zeros((t, QK_DIM - ROT_DIM), F32)
    c = jnp.concatenate([cos, cos, ones], axis=1)
    s1 = jnp.concatenate([zeros_h, sin, zeros_r], axis=1)
    s2 = jnp.concatenate([-sin, zeros_h, zeros_r], axis=1)
    rep = LANES // QK_DIM
    return tuple(jnp.tile(a, (1, rep)) for a in (c, s1, s2))


def _qk_norm_rope(acc, g, c, s1, s2):
    m, tn = acc.shape
    lane = lax.broadcasted_iota(jnp.int32, (m, LANES), 1)
    lo = lane < QK_DIM
    outs = []
    for t in range(tn // LANES):
        z = acc[:, t * LANES:(t + 1) * LANES]
        zz = z * z
        s_lo = jnp.sum(jnp.where(lo, zz, 0.0), axis=-1, keepdims=True)
        s_hi = jnp.sum(jnp.where(lo, 0.0, zz), axis=-1, keepdims=True)
        r = jnp.where(lo, lax.rsqrt(s_lo * (1.0 / QK_DIM) + EPS),
                      lax.rsqrt(s_hi * (1.0 / QK_DIM) + EPS))
        y = z * r * g
        y = (y * c + pltpu.roll(y, ROT_DIM // 2, 1) * s1
             + pltpu.roll(y, LANES - ROT_DIM // 2, 1) * s2)
        outs.append(y)
    return jnp.concatenate(outs, axis=1)


def _inproj_body(x_ref, sh_ref, sc_ref, w_ref, gq_ref, gk_ref, rc_ref, rs1_ref, rs2_ref,
                 q_ref, k_ref, v_ref, xb_ref, gb_ref, xn_scr, *, bb, tt, per_sec):
    j = pl.program_id(2)
    m = bb * tt
    d = x_ref.shape[-1]

    @pl.when(j == 0)
    def _():
        x = x_ref[...]
        ms = jnp.mean(x * x, axis=-1, keepdims=True)
        xn = x * lax.rsqrt(ms + EPS) * (1.0 + sc_ref[...]) + sh_ref[...]
        xn_scr[...] = xn.reshape(m, d).astype(BF16)

    acc = jnp.dot(xn_scr[...], w_ref[...], preferred_element_type=F32)
    sec = j // per_sec

    def tables():
        tabs = (rc_ref[...], rs1_ref[...], rs2_ref[...])
        if bb > 1:
            tabs = tuple(jnp.concatenate([a] * bb, axis=0) for a in tabs)
        return tabs

    @pl.when(sec == 0)
    def _():
        y = _qk_norm_rope(acc, gq_ref[...], *tables())
        q_ref[...] = (y * (QK_DIM ** -0.5)).astype(BF16).reshape(q_ref.shape)

    @pl.when(sec == 1)
    def _():
        k_ref[...] = _qk_norm_rope(acc, gk_ref[...], *tables()).reshape(k_ref.shape)

    @pl.when(sec == 2)
    def _():
        v_ref[...] = acc.reshape(v_ref.shape)

    @pl.when(sec == 3)
    def _():
        xb_ref[...] = acc.reshape(xb_ref.shape)

    @pl.when(sec == 4)
    def _():
        gb_ref[...] = acc.reshape(gb_ref.shape)


def _in_projection(x, mod, w_in, g_q, g_k, ropes, bb, tt, tn=512):
    b, t, d = x.shape
    n = w_in.shape[1]
    sec_w = n // 5
    per_sec = sec_w // tn
    rc, rs1, rs2 = ropes

    def sec_spec(s):
        return pl.BlockSpec(
            (bb, tt, tn),
            lambda bi, i, j, s=s: (bi, i, jnp.clip(j - s * per_sec, 0, per_sec - 1)))

    tab_spec = pl.BlockSpec((tt, LANES), lambda bi, i, j: (i, 0))
    g_spec = pl.BlockSpec((1, LANES), lambda bi, i, j: (0, 0))
    f32_out = jax.ShapeDtypeStruct((b, t, sec_w), F32)
    return pl.pallas_call(
        functools.partial(_inproj_body, bb=bb, tt=tt, per_sec=per_sec),
        grid=(b // bb, t // tt, n // tn),
        in_specs=[pl.BlockSpec((bb, tt, d), lambda bi, i, j: (bi, i, 0)),
                  _mod_spec(bb, d, 0, 3), _mod_spec(bb, d, 1, 3),
                  pl.BlockSpec((d, tn), lambda bi, i, j: (0, j)),
                  g_spec, g_spec, tab_spec, tab_spec, tab_spec],
        out_specs=[sec_spec(s) for s in range(5)],
        out_shape=[jax.ShapeDtypeStruct((b, t, sec_w), BF16), f32_out, f32_out, f32_out, f32_out],
        scratch_shapes=[pltpu.VMEM((bb * tt, d), BF16)],
        compiler_params=_params(3),
        name="in_projection",
    )(x, mod, mod, w_in, g_q, g_k, rc, rs1, rs2)


def _gelu_tanh(x):
    return 0.5 * x * (1.0 + jnp.tanh(math.sqrt(2.0 / math.pi) * (x + 0.044715 * (x * x * x))))


def _lru_body(xb_ref, gb_ref, h0_ref, c0_ref, cw_ref, cb_ref, wax_ref, ba_ref, bx_ref, lam_ref,
              y_ref, hn_ref, cn_ref, ext_scr, h_scr, *, tt):
    i = pl.program_id(1)
    w = xb_ref.shape[-1]
    blk = w // N_LRU_BLOCKS
    groups = tt // SUBLANES

    @pl.when(i == 0)
    def _():
        ext_scr[0:SUBLANES, :] = c0_ref[0]
        h_scr[...] = h0_ref[0]

    ext_scr[SUBLANES:, :] = xb_ref[0]
    u = cb_ref[...] + cw_ref[0:1, :] * ext_scr[pl.ds(SUBLANES - 3, tt), :]
    for jj in range(1, CONV_W):
        u = u + cw_ref[jj:jj + 1, :] * ext_scr[pl.ds(SUBLANES - 3 + jj, tt), :]
    tail = ext_scr[tt:tt + SUBLANES, :]
    ext_scr[0:SUBLANES, :] = tail

    ub = u.astype(BF16)
    ga, gx = [], []
    for n in range(N_LRU_BLOCKS):
        res = jnp.dot(ub[:, n * blk:(n + 1) * blk], wax_ref[n], preferred_element_type=F32)
        ga.append(res[:, :blk])
        gx.append(res[:, blk:])
    r = jax.nn.sigmoid(jnp.concatenate(ga, axis=1) + ba_ref[...])
    ig = jax.nn.sigmoid(jnp.concatenate(gx, axis=1) + bx_ref[...])

    nl = -lam_ref[...]
    softplus = jnp.maximum(nl, 0.0) + jnp.log1p(jnp.exp(-jnp.abs(nl)))
    log_a = -LRU_C * r * softplus
    a = jnp.exp(log_a)
    b = jnp.sqrt(-jnp.tanh(log_a) * (a * a + 1.0)) * (ig * u)

    a3 = a.reshape(groups, SUBLANES, w)
    b3 = b.reshape(groups, SUBLANES, w)
    row = lax.broadcasted_iota(jnp.int32, (groups, SUBLANES, w), 1)
    for s in (1, 2, 4):
        keep = row >= s
        b3 = jnp.where(keep, a3 * pltpu.roll(b3, s, 1) + b3, b3)
        a3 = jnp.where(keep, a3 * pltpu.roll(a3, s, 1), a3)
    h_prev = h_scr[...]
    hs = []
    for g in range(groups):
        hg = a3[g] * h_prev + b3[g]
        hs.append(hg)
        h_prev = hg[SUBLANES - 1:SUBLANES, :]
    h = jnp.concatenate(hs, axis=0)
    h_scr[...] = h_prev

    y_ref[0] = (h * _gelu_tanh(gb_ref[0])).astype(BF16)

    @pl.when(i == pl.num_programs(1) - 1)
    def _():
        hn_ref[0] = h_prev
        cn_ref[0] = tail


def _rglru(xb, gb, h0, conv0, conv_w, conv_b, w_ax, b_a, b_x, lru_lambda, tt):
    b, t, w = xb.shape
    blk = w // N_LRU_BLOCKS
    c0 = jnp.pad(conv0, ((0, 0), (SUBLANES - (CONV_W - 1), 0), (0, 0)))
    row_spec = pl.BlockSpec((1, tt, w), lambda bi, i: (bi, i, 0))
    vec_spec = pl.BlockSpec((1, w), lambda bi, i: (0, 0))
    y, hn, cn = pl.pallas_call(
        functools.partial(_lru_body, tt=tt),
        grid=(b, t // tt),
        in_specs=[row_spec, row_spec,
                  pl.BlockSpec((1, 1, w), lambda bi, i: (bi, 0, 0)),
                  pl.BlockSpec((1, SUBLANES, w), lambda bi, i: (bi, 0, 0)),
                  pl.BlockSpec((CONV_W, w), lambda bi, i: (0, 0)),
                  vec_spec,
                  pl.BlockSpec((N_LRU_BLOCKS, blk, 2 * blk), lambda bi, i: (0, 0, 0)),
                  vec_spec, vec_spec, vec_spec],
        out_specs=[row_spec,
                   pl.BlockSpec((1, 1, w), lambda bi, i: (bi, 0, 0)),
                   pl.BlockSpec((1, SUBLANES, w), lambda bi, i: (bi, 0, 0))],
        out_shape=[jax.ShapeDtypeStruct((b, t, w), BF16),
                   jax.ShapeDtypeStruct((b, 1, w), F32),
                   jax.ShapeDtypeStruct((b, SUBLANES, w), F32)],
        scratch_shapes=[pltpu.VMEM((tt + SUBLANES, w), F32), pltpu.VMEM((1, w), F32)],
        compiler_params=_params(2),
        name="rglru",
    )(xb, gb, h0.reshape(b, 1, w), c0, conv_w, conv_b.reshape(1, w), w_ax,
      b_a.reshape(1, w), b_x.reshape(1, w), lru_lambda.reshape(1, w))
    return y, hn[:, 0], cn[:, SUBLANES - (CONV_W - 1):]


def _lambda_value(lq1_ref, lk1_ref, lq2_ref, lk2_ref):
    s1 = jnp.sum(lq1_ref[...] * lk1_ref[...], axis=-1, keepdims=True)
    s2 = jnp.sum(lq2_ref[...] * lk2_ref[...], axis=-1, keepdims=True)
    return jnp.exp(s1) - jnp.exp(s2) + LAM_INIT


def _stack_components(q):
    lane = lax.broadcasted_iota(jnp.int32, q.shape, 1)
    zero = jnp.zeros_like(q)
    return jnp.concatenate([jnp.where(lane < QK_DIM, q, zero),
                            jnp.where(lane < QK_DIM, zero, q)], axis=0)


def _softmax_step(qs, kb, vb, m_scr, l_scr, acc_scr, mask=None):
    s = lax.dot_general(qs, kb, (((1,), (1,)), ((), ())), preferred_element_type=F32)
    if mask is not None:
        s = jnp.where(mask, s, -1e30)
    m_old = m_scr[...]
    m_new = jnp.maximum(m_old, jnp.max(s, axis=-1, keepdims=True))
    alpha = jnp.exp(m_old - m_new)
    p = jnp.exp(s - m_new)
    l_scr[...] = alpha * l_scr[...] + jnp.sum(p, axis=-1, keepdims=True)
    acc_scr[...] = alpha * acc_scr[...] + jnp.dot(p.astype(BF16), vb, preferred_element_type=F32)
    m_scr[...] = m_new


def _finish_attention(lam, g, l_scr, acc_scr, tq):
    o = acc_scr[...] / l_scr[...]
    o = o[:tq] - lam * o[tq:]
    ms = jnp.mean(o * o, axis=-1, keepdims=True)
    return (o * lax.rsqrt(ms + EPS) * g * (1.0 - LAM_INIT)).astype(BF16)


def _init_softmax(m_scr, l_scr, acc_scr):
    m_scr[...] = jnp.full(m_scr.shape, -jnp.inf, F32)
    l_scr[...] = jnp.zeros(l_scr.shape, F32)
    acc_scr[...] = jnp.zeros(acc_scr.shape, F32)


def _attn_prompt_body(lq1_ref, lk1_ref, lq2_ref, lk2_ref, g_ref, q_ref, k_ref, v_ref, o_ref,
                      kb_scr, vb_scr, m_scr, l_scr, acc_scr, *, tq):
    i = pl.program_id(2)

    @pl.when(i == 0)
    def _():
        kb_scr[...] = k_ref[0].astype(BF16)
        vb_scr[...] = v_ref[0].astype(BF16)

    qs = _stack_components(q_ref[0])
    _init_softmax(m_scr, l_scr, acc_scr)

    def visible_block(j, carry):
        start = pl.multiple_of(j * tq, tq)
        _softmax_step(qs, kb_scr[pl.ds(start, tq), :], vb_scr[pl.ds(start, tq), :],
                      m_scr, l_scr, acc_scr)
        return carry

    lax.fori_loop(0, i, visible_block, 0)

    rows = lax.broadcasted_iota(jnp.int32, (2 * tq, tq), 0)
    cols = lax.broadcasted_iota(jnp.int32, (2 * tq, tq), 1)
    q_chunk = jnp.where(rows >= tq, rows - tq, rows) // CHUNK
    start = pl.multiple_of(i * tq, tq)
    _softmax_step(qs, kb_scr[pl.ds(start, tq), :], vb_scr[pl.ds(start, tq), :],
                  m_scr, l_scr, acc_scr, mask=(cols // CHUNK) <= q_chunk)

    lam = _lambda_value(lq1_ref, lk1_ref, lq2_ref, lk2_ref)
    o_ref[0] = _finish_attention(lam, g_ref[...], l_scr, acc_scr, tq)


def _lambda_specs(n_axes):
    idx = (lambda b, h, i: (0, 0)) if n_axes == 3 else (lambda b, h: (0, 0))
    return [pl.BlockSpec((1, QK_DIM), idx)] * 4 + [pl.BlockSpec((1, V_DIM), idx)]


def _attention_prompt(q, k, v, lams, g_subln, tq=256):
    b, t, aw = q.shape
    head_q = pl.BlockSpec((1, tq, V_DIM), lambda bi, h, i: (bi, i, h))
    head_kv = pl.BlockSpec((1, t, V_DIM), lambda bi, h, i: (bi, 0, h))
    return pl.pallas_call(
        functools.partial(_attn_prompt_body, tq=tq),
        grid=(b, N_HEADS, t // tq),
        in_specs=_lambda_specs(3) + [head_q, head_kv, head_kv],
        out_specs=head_q,
        out_shape=jax.ShapeDtypeStruct((b, t, aw), BF16),
        scratch_shapes=[pltpu.VMEM((t, V_DIM), BF16), pltpu.VMEM((t, V_DIM), BF16),
                        pltpu.VMEM((2 * tq, 1), F32), pltpu.VMEM((2 * tq, 1), F32),
                        pltpu.VMEM((2 * tq, V_DIM), F32)],
        compiler_params=_params(3),
        name="attention_prompt",
    )(*lams, g_subln, q, k, v)


def _attn_sample_body(lq1_ref, lk1_ref, lq2_ref, lk2_ref, g_ref, q_ref, pk_ref, pv_ref,
                      nk_ref, nv_ref, o_ref, m_scr, l_scr, acc_scr, *, tq, tk):
    qs = _stack_components(q_ref[0])
    _init_softmax(m_scr, l_scr, acc_scr)

    def past_block(j, carry):
        start = pl.multiple_of(j * tk, tk)
        _softmax_step(qs, pk_ref[0, pl.ds(start, tk), :].astype(BF16),
                      pv_ref[0, pl.ds(start, tk), :].astype(BF16), m_scr, l_scr, acc_scr)
        return carry

    lax.fori_loop(0, pk_ref.shape[1] // tk, past_block, 0)
    _softmax_step(qs, nk_ref[0].astype(BF16), nv_ref[0].astype(BF16), m_scr, l_scr, acc_scr)

    lam = _lambda_value(lq1_ref, lk1_ref, lq2_ref, lk2_ref)
    o_ref[0] = _finish_attention(lam, g_ref[...], l_scr, acc_scr, tq)


def _attention_sample(q, past_k, past_v, new_k, new_v, lams, g_subln, tk=512):
    b, t, aw = q.shape
    past = past_k.shape[1]
    assert past % CHUNK == 0 and t <= CHUNK and past % tk == 0
    head_new = pl.BlockSpec((1, t, V_DIM), lambda bi, h: (bi, 0, h))
    head_past = pl.BlockSpec((1, past, V_DIM), lambda bi, h: (bi, 0, h))
    return pl.pallas_call(
        functools.partial(_attn_sample_body, tq=t, tk=tk),
        grid=(b, N_HEADS),
        in_specs=_lambda_specs(2) + [head_new, head_past, head_past, head_new, head_new],
        out_specs=head_new,
        out_shape=jax.ShapeDtypeStruct((b, t, aw), BF16),
        scratch_shapes=[pltpu.VMEM((2 * t, 1), F32), pltpu.VMEM((2 * t, 1), F32),
                        pltpu.VMEM((2 * t, V_DIM), F32)],
        compiler_params=_params(2),
        name="attention_sample",
    )(*lams, g_subln, q, past_k, past_v, new_k, new_v)


def _mlp_body(x_ref, o_ref, y_ref, gt1_ref, sh2_ref, sc2_ref, gt2_ref, wo_ref, wu_ref, wd_ref,
              out_ref, xn_scr, acc_scr, *, bb, tt):
    f = pl.program_id(2)
    m = bb * tt
    d = x_ref.shape[-1]
    aw = o_ref.shape[-1]

    @pl.when(f == 0)
    def _():
        mix = jnp.dot(o_ref[...].reshape(m, aw), wo_ref[0:aw, :], preferred_element_type=F32)
        mix = mix + jnp.dot(y_ref[...].reshape(m, y_ref.shape[-1]), wo_ref[aw:, :],
                            preferred_element_type=F32)
        x1 = x_ref[...] + gt1_ref[...] * mix.reshape(bb, tt, d)
        out_ref[...] = x1
        ms = jnp.mean(x1 * x1, axis=-1, keepdims=True)
        xn = x1 * lax.rsqrt(ms + EPS) * (1.0 + sc2_ref[...]) + sh2_ref[...]
        xn_scr[...] = xn.reshape(m, d).astype(BF16)
        acc_scr[...] = jnp.zeros(acc_scr.shape, F32)

    hid = jnp.maximum(jnp.dot(xn_scr[...], wu_ref[...], preferred_element_type=F32), 0.0)
    acc_scr[...] += jnp.dot((hid * hid).astype(BF16), wd_ref[...], preferred_element_type=F32)

    @pl.when(f == pl.num_programs(2) - 1)
    def _():
        out_ref[...] = out_ref[...] + gt2_ref[...] * acc_scr[...].reshape(bb, tt, d)


def _outproj_mlp(x, o, y, mod, w_out, w_up, w_down, bb, tt, tf=512):
    b, t, d = x.shape
    aw, lw = o.shape[-1], y.shape[-1]
    dff = w_up.shape[1]
    row = lambda width: pl.BlockSpec((bb, tt, width), lambda bi, i, f: (bi, i, 0))
    return pl.pallas_call(
        functools.partial(_mlp_body, bb=bb, tt=tt),
        grid=(b // bb, t // tt, dff // tf),
        in_specs=[row(d), row(aw), row(lw),
                  _mod_spec(bb, d, 2, 3), _mod_spec(bb, d, 3, 3),
                  _mod_spec(bb, d, 4, 3), _mod_spec(bb, d, 5, 3),
                  pl.BlockSpec((aw + lw, d), lambda bi, i, f: (0, 0),
                               pipeline_mode=pl.Buffered(1)),
                  pl.BlockSpec((d, tf), lambda bi, i, f: (0, f)),
                  pl.BlockSpec((tf, d), lambda bi, i, f: (f, 0))],
        out_specs=row(d),
        out_shape=jax.ShapeDtypeStruct((b, t, d), F32),
        scratch_shapes=[pltpu.VMEM((bb * tt, d), BF16), pltpu.VMEM((bb * tt, d), F32)],
        compiler_params=_params(3),
        name="outproj_mlp",
    )(x, o, y, mod, mod, mod, mod, w_out, w_up, w_down)


def _layer_common(x, mod, pos, h0, conv0, weights, bb, tt, lru_tt):
    (w_in, g_q, g_k, conv_w, conv_b, w_ax, b_a, b_x, lru_lambda) = weights
    q, k, v, xb, gb = _in_projection(x, mod, w_in, g_q, g_k, _rope_tables(pos), bb, tt)
    y, h_new, conv_new = _rglru(xb, gb, h0, conv0, conv_w, conv_b, w_ax, b_a, b_x, lru_lambda,
                                lru_tt)
    return q, k, v, y, h_new, conv_new


def kernel(x_prompt, x_sample, c_prompt, c_sample, cache_k, cache_v, state_lru_h, state_conv,
           w_ada, b_ada, w_in, g_q, g_k, lambda_q1, lambda_k1, lambda_q2, lambda_k2, g_subln,
           conv_w, conv_b, w_gate_a, b_gate_a, w_gate_x, b_gate_x, lru_lambda,
           w_out, w_up, w_down):
    bp, tp, d = x_prompt.shape
    bs, ts, _ = x_sample.shape
    past = cache_k.shape[2]
    aw = N_HEADS * V_DIM
    lw = conv_w.shape[-1]

    c_all = jnp.concatenate([c_prompt, c_sample], axis=0)
    mod = _modulation(c_all, w_ada[0], b_ada[0]).reshape(bp + bs, N_MOD, 1, d)
    mod_p, mod_s = mod[:bp], mod[bp:]

    rep = LANES // QK_DIM
    weights = (w_in[0].astype(BF16),
               jnp.tile(g_q[0], rep).reshape(1, LANES), jnp.tile(g_k[0], rep).reshape(1, LANES),
               conv_w[0], conv_b[0],
               jnp.concatenate([w_gate_a[0], w_gate_x[0]], axis=-1).astype(BF16),
               b_gate_a[0].reshape(-1), b_gate_x[0].reshape(-1), lru_lambda[0])
    lams = tuple(a[0].reshape(1, QK_DIM) for a in (lambda_q1, lambda_k1, lambda_q2, lambda_k2))
    g_sub = g_subln[0].reshape(1, V_DIM)
    wo, wu, wd = w_out[0].astype(BF16), w_up[0].astype(BF16), w_down[0].astype(BF16)

    tt_p = min(tp, 512)
    qp, kp, vp, yp, hp, cp = _layer_common(
        x_prompt, mod_p, jnp.arange(tp), jnp.zeros((bp, lw), F32),
        jnp.zeros((bp, CONV_W - 1, lw), F32), weights, 1, tt_p, min(tp, 256))
    op = _attention_prompt(qp, kp, vp, lams, g_sub, tq=min(tp, 256))
    out_p = _outproj_mlp(x_prompt, op, yp, mod_p, wo, wu, wd, 1, tt_p)

    qs, ks, vs, ys, hs, cs = _layer_common(
        x_sample, mod_s, past + jnp.arange(ts), state_lru_h[0], state_conv[0], weights,
        bs, ts, ts)
    osamp = _attention_sample(qs, cache_k[0].reshape(bs, past, aw), cache_v[0].reshape(bs, past, aw),
                              ks, vs, lams, g_sub, tk=min(past, 512))
    out_s = _outproj_mlp(x_sample, osamp, ys, mod_s, wo, wu, wd, bs, ts)

    return (out_p, out_s,
            kp.reshape(1, bp, tp, N_HEADS, 2, QK_DIM), vp.reshape(1, bp, tp, N_HEADS, V_DIM),
            hp[None], cp[None],
            ks.reshape(1, bs, ts, N_HEADS, 2, QK_DIM), vs.reshape(1, bs, ts, N_HEADS, V_DIM),
            hs[None], cs[None])
```

```python
import functools
import math

import jax
import jax.numpy as jnp
from jax import lax
from jax.experimental import pallas as pl
from jax.experimental.pallas import tpu as pltpu

F32 = jnp.float32
BF16 = jnp.bfloat16

LANES = 128
SUBLANES = 8
VMEM_LIMIT = 56 * 1024 * 1024

N_HEADS = 8
QK_DIM = 64
V_DIM = 2 * QK_DIM
CHUNK = 64
ROT_DIM = QK_DIM // 4
ROPE_THETA = 500000.0
N_LRU_BLOCKS = 8
CONV_W = 4
LRU_C = 8.0
N_MOD = 6
EPS = 1e-6
LAM_INIT = 0.8 - 0.6 * math.exp(-0.3 * 0)


def _params(n_axes):
    return pltpu.CompilerParams(dimension_semantics=("arbitrary",) * n_axes,
                                vmem_limit_bytes=VMEM_LIMIT)


def _mod_body(c_ref, w_ref, b_ref, o_ref):
    c = c_ref[...]
    s = c * jax.nn.sigmoid(c)
    o_ref[...] = jnp.dot(s.astype(BF16), w_ref[...].astype(BF16),
                         preferred_element_type=F32) + b_ref[...]


def _modulation(c, w_ada, b_ada, tn=1024):
    rows, d = c.shape
    n = w_ada.shape[1]
    return pl.pallas_call(
        _mod_body,
        grid=(n // tn,),
        in_specs=[pl.BlockSpec((rows, d), lambda j: (0, 0)),
                  pl.BlockSpec((d, tn), lambda j: (0, j)),
                  pl.BlockSpec((1, tn), lambda j: (0, j))],
        out_specs=pl.BlockSpec((rows, tn), lambda j: (0, j)),
        out_shape=jax.ShapeDtypeStruct((rows, n), F32),
        compiler_params=_params(1),
        name="modulation",
    )(c, w_ada, b_ada.reshape(1, n))


def _mod_spec(bb, d, which, n_axes):
    if n_axes == 3:
        return pl.BlockSpec((bb, None, 1, d), lambda b, i, j: (b, which, 0, 0))
    return pl.BlockSpec((bb, None, 1, d), lambda b, i: (b, which, 0, 0))


def _rope_tables(pos):
    half = ROT_DIM // 2
    inv_freq = ROPE_THETA ** (-(jnp.arange(half, dtype=F32) * 2.0) / ROT_DIM)
    ang = pos.astype(F32)[:, None] * inv_freq[None, :]
    cos, sin = jnp.cos(ang), jnp.sin(ang)
    t = pos.shape[0]
    ones = jnp.ones((t, QK_DIM - ROT_DIM), F32)
    zeros_h = jnp.zeros((t, half), F32)
    zeros_r = jnp.zeros((t, QK_DIM - ROT_DIM), F32)
    c = jnp.concatenate([cos, cos, ones], axis=1)
    s1 = jnp.concatenate([zeros_h, sin, zeros_r], axis=1)
    s2 = jnp.concatenate([-sin, zeros_h, zeros_r], axis=1)
    rep = LANES // QK_DIM
    return tuple(jnp.tile(a, (1, rep)) for a in (c, s1, s2))


def _qk_norm_rope(acc, g, c, s1, s2):
    m, tn = acc.shape
    lane = lax.broadcasted_iota(jnp.int32, (m, LANES), 1)
    lo = lane < QK_DIM
    outs = []
    for t in range(tn // LANES):
        z = acc[:, t * LANES:(t + 1) * LANES]
        zz = z * z
        s_lo = jnp.sum(jnp.where(lo, zz, 0.0), axis=-1, keepdims=True)
        s_hi = jnp.sum(jnp.where(lo, 0.0, zz), axis=-1, keepdims=True)
        r = jnp.where(lo, lax.rsqrt(s_lo * (1.0 / QK_DIM) + EPS),
                      lax.rsqrt(s_hi * (1.0 / QK_DIM) + EPS))
        y = z * r * g
        y = (y * c + pltpu.roll(y, ROT_DIM // 2, 1) * s1
             + pltpu.roll(y, LANES - ROT_DIM // 2, 1) * s2)
        outs.append(y)
    return jnp.concatenate(outs, axis=1)


def _inproj_body(x_ref, sh_ref, sc_ref, w_ref, gq_ref, gk_ref, rc_ref, rs1_ref, rs2_ref,
                 q_ref, k_ref, v_ref, xb_ref, gb_ref, xn_scr, *, bb, tt, tn):
    m = bb * tt
    d = x_ref.shape[-1]
    x = x_ref[...]
    ms = jnp.mean(x * x, axis=-1, keepdims=True)
    xn = x * lax.rsqrt(ms + EPS) * (1.0 + sc_ref[...]) + sh_ref[...]
    xn_scr[...] = xn.reshape(m, d).astype(BF16)

    tabs = (rc_ref[...], rs1_ref[...], rs2_ref[...])
    if bb > 1:
        tabs = tuple(jnp.concatenate([a] * bb, axis=0) for a in tabs)

    sec_w = q_ref.shape[-1]
    for s, out_ref in enumerate((q_ref, k_ref, v_ref, xb_ref, gb_ref)):
        for h in range(sec_w // tn):
            col = s * sec_w + h * tn
            acc = jnp.dot(xn_scr[...], w_ref[:, col:col + tn], preferred_element_type=F32)
            if s == 0:
                acc = (_qk_norm_rope(acc, gq_ref[...], *tabs) * (QK_DIM ** -0.5)).astype(BF16)
            elif s == 1:
                acc = _qk_norm_rope(acc, gk_ref[...], *tabs)
            out_ref[:, :, h * tn:(h + 1) * tn] = acc.reshape(bb, tt, tn)


def _in_projection(x, mod, w_in, g_q, g_k, ropes, bb, tt, tn=512):
    b, t, d = x.shape
    n = w_in.shape[1]
    sec_w = n // 5
    rc, rs1, rs2 = ropes
    sec_spec = pl.BlockSpec((bb, tt, sec_w), lambda bi, i: (bi, i, 0))
    tab_spec = pl.BlockSpec((tt, LANES), lambda bi, i: (i, 0))
    g_spec = pl.BlockSpec((1, LANES), lambda bi, i: (0, 0))
    f32_out = jax.ShapeDtypeStruct((b, t, sec_w), F32)
    return pl.pallas_call(
        functools.partial(_inproj_body, bb=bb, tt=tt, tn=tn),
        grid=(b // bb, t // tt),
        in_specs=[pl.BlockSpec((bb, tt, d), lambda bi, i: (bi, i, 0)),
                  _mod_spec(bb, d, 0, 2), _mod_spec(bb, d, 1, 2),
                  pl.BlockSpec((d, n), lambda bi, i: (0, 0), pipeline_mode=pl.Buffered(1)),
                  g_spec, g_spec, tab_spec, tab_spec, tab_spec],
        out_specs=[sec_spec] * 5,
        out_shape=[jax.ShapeDtypeStruct((b, t, sec_w), BF16), f32_out, f32_out, f32_out, f32_out],
        scratch_shapes=[pltpu.VMEM((bb * tt, d), BF16)],
        compiler_params=_params(2),
        name="in_projection",
    )(x, mod, mod, w_in, g_q, g_k, rc, rs1, rs2)


def _gelu_tanh(x):
    return 0.5 * x * (1.0 + jnp.tanh(math.sqrt(2.0 / math.pi) * (x + 0.044715 * (x * x * x))))


def _lru_body(xb_ref, gb_ref, h0_ref, c0_ref, cw_ref, cb_ref, wax_ref, ba_ref, bx_ref, lam_ref,
              y_ref, hn_ref, cn_ref, ext_scr, h_scr, *, tt):
    i = pl.program_id(1)
    w = xb_ref.shape[-1]
    blk = w // N_LRU_BLOCKS
    groups = tt // SUBLANES

    @pl.when(i == 0)
    def _():
        ext_scr[0:SUBLANES, :] = c0_ref[0]
        h_scr[...] = h0_ref[0]

    ext_scr[SUBLANES:, :] = xb_ref[0]
    u = cb_ref[...] + cw_ref[0:1, :] * ext_scr[pl.ds(SUBLANES - 3, tt), :]
    for jj in range(1, CONV_W):
        u = u + cw_ref[jj:jj + 1, :] * ext_scr[pl.ds(SUBLANES - 3 + jj, tt), :]
    tail = ext_scr[tt:tt + SUBLANES, :]
    ext_scr[0:SUBLANES, :] = tail

    ub = u.astype(BF16)
    ga, gx = [], []
    for n in range(N_LRU_BLOCKS):
        res = jnp.dot(ub[:, n * blk:(n + 1) * blk], wax_ref[n], preferred_element_type=F32)
        ga.append(res[:, :blk])
        gx.append(res[:, blk:])
    r = jax.nn.sigmoid(jnp.concatenate(ga, axis=1) + ba_ref[...])
    ig = jax.nn.sigmoid(jnp.concatenate(gx, axis=1) + bx_ref[...])

    nl = -lam_ref[...]
    softplus = jnp.maximum(nl, 0.0) + jnp.log1p(jnp.exp(-jnp.abs(nl)))
    log_a = -LRU_C * r * softplus
    a = jnp.exp(log_a)
    b = jnp.sqrt(-jnp.tanh(log_a) * (a * a + 1.0)) * (ig * u)

    a3 = a.reshape(groups, SUBLANES, w)
    b3 = b.reshape(groups, SUBLANES, w)
    row = lax.broadcasted_iota(jnp.int32, (groups, SUBLANES, w), 1)
    for s in (1, 2, 4):
        keep = row >= s
        b3 = jnp.where(keep, a3 * pltpu.roll(b3, s, 1) + b3, b3)
        a3 = jnp.where(keep, a3 * pltpu.roll(a3, s, 1), a3)
    h_prev = h_scr[...]
    hs = []
    for g in range(groups):
        hg = a3[g] * h_prev + b3[g]
        hs.append(hg)
        h_prev = hg[SUBLANES - 1:SUBLANES, :]
    h = jnp.concatenate(hs, axis=0)
    h_scr[...] = h_prev

    y_ref[0] = (h * _gelu_tanh(gb_ref[0])).astype(BF16)

    @pl.when(i == pl.num_programs(1) - 1)
    def _():
        hn_ref[0] = h_prev
        cn_ref[0] = tail


def _rglru(xb, gb, h0, conv0, conv_w, conv_b, w_ax, b_a, b_x, lru_lambda, tt):
    b, t, w = xb.shape
    blk = w // N_LRU_BLOCKS
    c0 = jnp.pad(conv0, ((0, 0), (SUBLANES - (CONV_W - 1), 0), (0, 0)))
    row_spec = pl.BlockSpec((1, tt, w), lambda bi, i: (bi, i, 0))
    vec_spec = pl.BlockSpec((1, w), lambda bi, i: (0, 0))
    y, hn, cn = pl.pallas_call(
        functools.partial(_lru_body, tt=tt),
        grid=(b, t // tt),
        in_specs=[row_spec, row_spec,
                  pl.BlockSpec((1, 1, w), lambda bi, i: (bi, 0, 0)),
                  pl.BlockSpec((1, SUBLANES, w), lambda bi, i: (bi, 0, 0)),
                  pl.BlockSpec((CONV_W, w), lambda bi, i: (0, 0)),
                  vec_spec,
                  pl.BlockSpec((N_LRU_BLOCKS, blk, 2 * blk), lambda bi, i: (0, 0, 0)),
                  vec_spec, vec_spec, vec_spec],
        out_specs=[row_spec,
                   pl.BlockSpec((1, 1, w), lambda bi, i: (bi, 0, 0)),
                   pl.BlockSpec((1, SUBLANES, w), lambda bi, i: (bi, 0, 0))],
        out_shape=[jax.ShapeDtypeStruct((b, t, w), BF16),
                   jax.ShapeDtypeStruct((b, 1, w), F32),
                   jax.ShapeDtypeStruct((b, SUBLANES, w), F32)],
        scratch_shapes=[pltpu.VMEM((tt + SUBLANES, w), F32), pltpu.VMEM((1, w), F32)],
        compiler_params=_params(2),
        name="rglru",
    )(xb, gb, h0.reshape(b, 1, w), c0, conv_w, conv_b.reshape(1, w), w_ax,
      b_a.reshape(1, w), b_x.reshape(1, w), lru_lambda.reshape(1, w))
    return y, hn[:, 0], cn[:, SUBLANES - (CONV_W - 1):]


def _lambda_value(lq1_ref, lk1_ref, lq2_ref, lk2_ref):
    s1 = jnp.sum(lq1_ref[...] * lk1_ref[...], axis=-1, keepdims=True)
    s2 = jnp.sum(lq2_ref[...] * lk2_ref[...], axis=-1, keepdims=True)
    return jnp.exp(s1) - jnp.exp(s2) + LAM_INIT


def _stack_components(q):
    lane = lax.broadcasted_iota(jnp.int32, q.shape, 1)
    zero = jnp.zeros_like(q)
    return jnp.concatenate([jnp.where(lane < QK_DIM, q, zero),
                            jnp.where(lane < QK_DIM, zero, q)], axis=0)


def _softmax_step(qs, kb, vb, m_scr, l_scr, acc_scr, mask=None):
    s = lax.dot_general(qs, kb, (((1,), (1,)), ((), ())), preferred_element_type=F32)
    if mask is not None:
        s = jnp.where(mask, s, -1e30)
    m_old = m_scr[...]
    m_new = jnp.maximum(m_old, jnp.max(s, axis=-1, keepdims=True))
    alpha = jnp.exp(m_old - m_new)
    p = jnp.exp(s - m_new)
    l_scr[...] = alpha * l_scr[...] + jnp.sum(p, axis=-1, keepdims=True)
    acc_scr[...] = alpha * acc_scr[...] + jnp.dot(p.astype(BF16), vb, preferred_element_type=F32)
    m_scr[...] = m_new


def _finish_attention(lam, g, l_scr, acc_scr, tq):
    o = acc_scr[...] / l_scr[...]
    o = o[:tq] - lam * o[tq:]
    ms = jnp.mean(o * o, axis=-1, keepdims=True)
    return (o * lax.rsqrt(ms + EPS) * g * (1.0 - LAM_INIT)).astype(BF16)


def _init_softmax(m_scr, l_scr, acc_scr):
    m_scr[...] = jnp.full(m_scr.shape, -jnp.inf, F32)
    l_scr[...] = jnp.zeros(l_scr.shape, F32)
    acc_scr[...] = jnp.zeros(acc_scr.shape, F32)


def _attn_prompt_body(lq1_ref, lk1_ref, lq2_ref, lk2_ref, g_ref, q_ref, k_ref, v_ref, o_ref,
                      kb_scr, vt_scr, s_scr, p_scr, *, tq):
    t = q_ref.shape[1]
    kb_scr[...] = k_ref[0].astype(BF16)
    vt_scr[...] = v_ref[0].T.astype(BF16)
    lam = _lambda_value(lq1_ref, lk1_ref, lq2_ref, lk2_ref)
    gain = g_ref[...] * (1.0 - LAM_INIT)

    key_chunk = lax.broadcasted_iota(jnp.int32, (tq, 2 * tq), 0) // CHUNK
    qcol = lax.broadcasted_iota(jnp.int32, (tq, 2 * tq), 1)
    q_chunk = jnp.where(qcol >= tq, qcol - tq, qcol) // CHUNK
    visible = key_chunk <= q_chunk

    def fold(x):
        return x.reshape(tq // SUBLANES, SUBLANES, x.shape[-1])

    for i in range(t // tq):
        qs = _stack_components(q_ref[0, i * tq:(i + 1) * tq, :])
        m8 = jnp.full((SUBLANES, 2 * tq), -jnp.inf, F32)
        for c in range(i + 1):
            sc = lax.dot_general(kb_scr[c * tq:(c + 1) * tq, :], qs, (((1,), (1,)), ((), ())),
                                 preferred_element_type=F32)
            if c == i:
                sc = jnp.where(visible, sc, -1e30)
            s_scr[c * tq:(c + 1) * tq, :] = sc
            m8 = jnp.maximum(m8, jnp.max(fold(sc), axis=0))
        m = jnp.max(m8, axis=0, keepdims=True)
        l8 = jnp.zeros((SUBLANES, 2 * tq), F32)
        for c in range(i + 1):
            p = jnp.exp(s_scr[c * tq:(c + 1) * tq, :] - m)
            l8 = l8 + jnp.sum(fold(p), axis=0)
            p_scr[c * tq:(c + 1) * tq, :] = p.astype(BF16)
        l = jnp.sum(l8, axis=0, keepdims=True)
        n_keys = (i + 1) * tq
        ot = jnp.dot(vt_scr[:, 0:n_keys], p_scr[0:n_keys, :], preferred_element_type=F32)
        ot = ot / l
        o = ot[:, :tq] - lam * ot[:, tq:]
        ms = jnp.mean(o * o, axis=0, keepdims=True)
        o = o * lax.rsqrt(ms + EPS) * gain
        o_ref[0, i * tq:(i + 1) * tq, :] = o.T.astype(BF16)


def _attention_prompt(q, k, v, lams, g_subln, tq=256):
    b, t, aw = q.shape
    head = pl.BlockSpec((1, t, V_DIM), lambda bi, h: (bi, 0, h))
    lam_spec = pl.BlockSpec((1, QK_DIM), lambda bi, h: (0, 0))
    return pl.pallas_call(
        functools.partial(_attn_prompt_body, tq=tq),
        grid=(b, N_HEADS),
        in_specs=[lam_spec] * 4 + [pl.BlockSpec((V_DIM, 1), lambda bi, h: (0, 0)),
                                   head, head, head],
        out_specs=head,
        out_shape=jax.ShapeDtypeStruct((b, t, aw), BF16),
        scratch_shapes=[pltpu.VMEM((t, V_DIM), BF16), pltpu.VMEM((V_DIM, t), BF16),
                        pltpu.VMEM((t, 2 * tq), F32), pltpu.VMEM((t, 2 * tq), BF16)],
        compiler_params=_params(2),
        name="attention_prompt",
    )(*lams, g_subln.reshape(V_DIM, 1), q, k, v)


def _attn_sample_body(lq1_ref, lk1_ref, lq2_ref, lk2_ref, g_ref, q_ref, pk_ref, pv_ref,
                      nk_ref, nv_ref, o_ref, m_scr, l_scr, acc_scr, *, tq, tk):
    qs = _stack_components(q_ref[0])
    _init_softmax(m_scr, l_scr, acc_scr)

    def past_block(j, carry):
        start = pl.multiple_of(j * tk, tk)
        _softmax_step(qs, pk_ref[0, pl.ds(start, tk), :].astype(BF16),
                      pv_ref[0, pl.ds(start, tk), :].astype(BF16), m_scr, l_scr, acc_scr)
        return carry

    lax.fori_loop(0, pk_ref.shape[1] // tk, past_block, 0)
    _softmax_step(qs, nk_ref[0].astype(BF16), nv_ref[0].astype(BF16), m_scr, l_scr, acc_scr)

    lam = _lambda_value(lq1_ref, lk1_ref, lq2_ref, lk2_ref)
    o_ref[0] = _finish_attention(lam, g_ref[...], l_scr, acc_scr, tq)


def _attention_sample(q, past_k, past_v, new_k, new_v, lams, g_subln, tk=512):
    b, t, aw = q.shape
    past = past_k.shape[1]
    assert past % CHUNK == 0 and t <= CHUNK and past % tk == 0
    head_new = pl.BlockSpec((1, t, V_DIM), lambda bi, h: (bi, 0, h))
    head_past = pl.BlockSpec((1, past, V_DIM), lambda bi, h: (bi, 0, h))
    vec = lambda width: pl.BlockSpec((1, width), lambda bi, h: (0, 0))
    return pl.pallas_call(
        functools.partial(_attn_sample_body, tq=t, tk=tk),
        grid=(b, N_HEADS),
        in_specs=[vec(QK_DIM)] * 4 + [vec(V_DIM), head_new, head_past, head_past,
                                      head_new, head_new],
        out_specs=head_new,
        out_shape=jax.ShapeDtypeStruct((b, t, aw), BF16),
        scratch_shapes=[pltpu.VMEM((2 * t, 1), F32), pltpu.VMEM((2 * t, 1), F32),
                        pltpu.VMEM((2 * t, V_DIM), F32)],
        compiler_params=_params(2),
        name="attention_sample",
    )(*lams, g_subln, q, past_k, past_v, new_k, new_v)


def _mlp_body(x_ref, o_ref, y_ref, gt1_ref, sh2_ref, sc2_ref, gt2_ref, wo_ref, wu_ref, wd_ref,
              out_ref, xn_scr, acc_scr, *, bb, tt):
    f = pl.program_id(2)
    m = bb * tt
    d = x_ref.shape[-1]
    aw = o_ref.shape[-1]

    @pl.when(f == 0)
    def _():
        mix = jnp.dot(o_ref[...].reshape(m, aw), wo_ref[0:aw, :], preferred_element_type=F32)
        mix = mix + jnp.dot(y_ref[...].reshape(m, y_ref.shape[-1]), wo_ref[aw:, :],
                            preferred_element_type=F32)
        x1 = x_ref[...] + gt1_ref[...] * mix.reshape(bb, tt, d)
        out_ref[...] = x1
        ms = jnp.mean(x1 * x1, axis=-1, keepdims=True)
        xn = x1 * lax.rsqrt(ms + EPS) * (1.0 + sc2_ref[...]) + sh2_ref[...]
        xn_scr[...] = xn.reshape(m, d).astype(BF16)
        acc_scr[...] = jnp.zeros(acc_scr.shape, F32)

    hid = jnp.maximum(jnp.dot(xn_scr[...], wu_ref[...], preferred_element_type=F32), 0.0)
    acc_scr[...] += jnp.dot((hid * hid).astype(BF16), wd_ref[...], preferred_element_type=F32)

    @pl.when(f == pl.num_programs(2) - 1)
    def _():
        out_ref[...] = out_ref[...] + gt2_ref[...] * acc_scr[...].reshape(bb, tt, d)


def _outproj_mlp(x, o, y, mod, w_out, w_up, w_down, bb, tt, tf=1024):
    b, t, d = x.shape
    aw, lw = o.shape[-1], y.shape[-1]
    dff = w_up.shape[1]
    row = lambda width: pl.BlockSpec((bb, tt, width), lambda bi, i, f: (bi, i, 0))
    return pl.pallas_call(
        functools.partial(_mlp_body, bb=bb, tt=tt),
        grid=(b // bb, t // tt, dff // tf),
        in_specs=[row(d), row(aw), row(lw),
                  _mod_spec(bb, d, 2, 3), _mod_spec(bb, d, 3, 3),
                  _mod_spec(bb, d, 4, 3), _mod_spec(bb, d, 5, 3),
                  pl.BlockSpec((aw + lw, d), lambda bi, i, f: (0, 0),
                               pipeline_mode=pl.Buffered(1)),
                  pl.BlockSpec((d, tf), lambda bi, i, f: (0, f)),
                  pl.BlockSpec((tf, d), lambda bi, i, f: (f, 0))],
        out_specs=row(d),
        out_shape=jax.ShapeDtypeStruct((b, t, d), F32),
        scratch_shapes=[pltpu.VMEM((bb * tt, d), BF16), pltpu.VMEM((bb * tt, d), F32)],
        compiler_params=_params(3),
        name="outproj_mlp",
    )(x, o, y, mod, mod, mod, mod, w_out, w_up, w_down)


def _layer_common(x, mod, pos, h0, conv0, weights, bb, tt, lru_tt):
    (w_in, g_q, g_k, conv_w, conv_b, w_ax, b_a, b_x, lru_lambda) = weights
    q, k, v, xb, gb = _in_projection(x, mod, w_in, g_q, g_k, _rope_tables(pos), bb, tt)
    y, h_new, conv_new = _rglru(xb, gb, h0, conv0, conv_w, conv_b, w_ax, b_a, b_x, lru_lambda,
                                lru_tt)
    return q, k, v, y, h_new, conv_new


def kernel(x_prompt, x_sample, c_prompt, c_sample, cache_k, cache_v, state_lru_h, state_conv,
           w_ada, b_ada, w_in, g_q, g_k, lambda_q1, lambda_k1, lambda_q2, lambda_k2, g_subln,
           conv_w, conv_b, w_gate_a, b_gate_a, w_gate_x, b_gate_x, lru_lambda,
           w_out, w_up, w_down):
    bp, tp, d = x_prompt.shape
    bs, ts, _ = x_sample.shape
    past = cache_k.shape[2]
    aw = N_HEADS * V_DIM
    lw = conv_w.shape[-1]

    c_all = jnp.concatenate([c_prompt, c_sample], axis=0)
    mod = _modulation(c_all, w_ada[0], b_ada[0]).reshape(bp + bs, N_MOD, 1, d)
    mod_p, mod_s = mod[:bp], mod[bp:]

    rep = LANES // QK_DIM
    weights = (w_in[0].astype(BF16),
               jnp.tile(g_q[0], rep).reshape(1, LANES), jnp.tile(g_k[0], rep).reshape(1, LANES),
               conv_w[0], conv_b[0],
               jnp.concatenate([w_gate_a[0], w_gate_x[0]], axis=-1).astype(BF16),
               b_gate_a[0].reshape(-1), b_gate_x[0].reshape(-1), lru_lambda[0])
    lams = tuple(a[0].reshape(1, QK_DIM) for a in (lambda_q1, lambda_k1, lambda_q2, lambda_k2))
    g_sub = g_subln[0].reshape(1, V_DIM)
    wo, wu, wd = w_out[0].astype(BF16), w_up[0].astype(BF16), w_down[0].astype(BF16)

    tt_p = min(tp, 512)
    qp, kp, vp, yp, hp, cp = _layer_common(
        x_prompt, mod_p, jnp.arange(tp), jnp.zeros((bp, lw), F32),
        jnp.zeros((bp, CONV_W - 1, lw), F32), weights, 1, tt_p, min(tp, 256))
    op = _attention_prompt(qp, kp, vp, lams, g_sub, tq=min(tp, 256))
    out_p = _outproj_mlp(x_prompt, op, yp, mod_p, wo, wu, wd, 1, tt_p)

    qs, ks, vs, ys, hs, cs = _layer_common(
        x_sample, mod_s, past + jnp.arange(ts), state_lru_h[0], state_conv[0], weights,
        bs, ts, ts)
    osamp = _attention_sample(qs, cache_k[0].reshape(bs, past, aw), cache_v[0].reshape(bs, past, aw),
                              ks, vs, lams, g_sub, tk=min(past, 512))
    out_s = _outproj_mlp(x_sample, osamp, ys, mod_s, wo, wu, wd, bs, ts)

    return (out_p, out_s,
            kp.reshape(1, bp, tp, N_HEADS, 2, QK_DIM), vp.reshape(1, bp, tp, N_HEADS, V_DIM),
            hp[None], cp[None],
            ks.reshape(1, bs, ts, N_HEADS, 2, QK_DIM), vs.reshape(1, bs, ts, N_HEADS, V_DIM),
            hs[None], cs[None])
```

```python
import functools
import math

import jax
import jax.numpy as jnp
from jax import lax
from jax.experimental import pallas as pl
from jax.experimental.pallas import tpu as pltpu

F32 = jnp.float32
BF16 = jnp.bfloat16

LANES = 128
SUBLANES = 8
VMEM_LIMIT = 56 * 1024 * 1024

N_HEADS = 8
QK_DIM = 64
V_DIM = 2 * QK_DIM
CHUNK = 64
ROT_DIM = QK_DIM // 4
ROPE_THETA = 500000.0
N_LRU_BLOCKS = 8
CONV_W = 4
LRU_C = 8.0
N_MOD = 6
EPS = 1e-6
LAM_INIT = 0.8 - 0.6 * math.exp(-0.3 * 0)
LOG2E = math.log2(math.e)


def _params(n_axes):
    return pltpu.CompilerParams(dimension_semantics=("arbitrary",) * n_axes,
                                vmem_limit_bytes=VMEM_LIMIT)


def _mod_body(c_ref, w_ref, b_ref, o_ref):
    c = c_ref[...]
    s = c * jax.nn.sigmoid(c)
    o_ref[...] = jnp.dot(s.astype(BF16), w_ref[...].astype(BF16),
                         preferred_element_type=F32) + b_ref[...]


def _modulation(c, w_ada, b_ada, tn=1024):
    rows, d = c.shape
    n = w_ada.shape[1]
    return pl.pallas_call(
        _mod_body,
        grid=(n // tn,),
        in_specs=[pl.BlockSpec((rows, d), lambda j: (0, 0)),
                  pl.BlockSpec((d, tn), lambda j: (0, j)),
                  pl.BlockSpec((1, tn), lambda j: (0, j))],
        out_specs=pl.BlockSpec((rows, tn), lambda j: (0, j)),
        out_shape=jax.ShapeDtypeStruct((rows, n), F32),
        compiler_params=_params(1),
        name="modulation",
    )(c, w_ada, b_ada.reshape(1, n))


def _mod_spec(bb, d, which, n_axes):
    if n_axes == 3:
        return pl.BlockSpec((bb, None, 1, d), lambda b, i, j: (b, which, 0, 0))
    return pl.BlockSpec((bb, None, 1, d), lambda b, i: (b, which, 0, 0))


def _rope_tables(pos):
    half = ROT_DIM // 2
    inv_freq = ROPE_THETA ** (-(jnp.arange(half, dtype=F32) * 2.0) / ROT_DIM)
    ang = pos.astype(F32)[:, None] * inv_freq[None, :]
    cos, sin = jnp.cos(ang), jnp.sin(ang)
    t = pos.shape[0]
    ones = jnp.ones((t, QK_DIM - ROT_DIM), F32)
    zeros_h = jnp.zeros((t, half), F32)
    zeros_r = jnp.zeros((t, QK_DIM - ROT_DIM), F32)
    c = jnp.concatenate([cos, cos, ones], axis=1)
    s1 = jnp.concatenate([zeros_h, sin, zeros_r], axis=1)
    s2 = jnp.concatenate([-sin, zeros_h, zeros_r], axis=1)
    rep = LANES // QK_DIM
    return tuple(jnp.tile(a, (1, rep)) for a in (c, s1, s2))


def _qk_norm_rope(acc, g, c, s1, s2):
    m, tn = acc.shape
    lane = lax.broadcasted_iota(jnp.int32, (m, LANES), 1)
    lo = lane < QK_DIM
    outs = []
    for t in range(tn // LANES):
        z = acc[:, t * LANES:(t + 1) * LANES]
        zz = z * z
        s_lo = jnp.sum(jnp.where(lo, zz, 0.0), axis=-1, keepdims=True)
        s_hi = jnp.sum(jnp.where(lo, 0.0, zz), axis=-1, keepdims=True)
        r = jnp.where(lo, lax.rsqrt(s_lo * (1.0 / QK_DIM) + EPS),
                      lax.rsqrt(s_hi * (1.0 / QK_DIM) + EPS))
        y = z * r * g
        y = (y * c + pltpu.roll(y, ROT_DIM // 2, 1) * s1
             + pltpu.roll(y, LANES - ROT_DIM // 2, 1) * s2)
        outs.append(y)
    return jnp.concatenate(outs, axis=1)


def _inproj_body(x_ref, sh_ref, sc_ref, w_ref, gq_ref, gk_ref, rc_ref, rs1_ref, rs2_ref,
                 q_ref, k_ref, v_ref, xb_ref, gb_ref, xn_scr, *, bb, tt, tn):
    m = bb * tt
    d = x_ref.shape[-1]
    x = x_ref[...]
    ms = jnp.mean(x * x, axis=-1, keepdims=True)
    xn = x * lax.rsqrt(ms + EPS) * (1.0 + sc_ref[...]) + sh_ref[...]
    xn_scr[...] = xn.reshape(m, d).astype(BF16)

    tabs = (rc_ref[...], rs1_ref[...], rs2_ref[...])
    if bb > 1:
        tabs = tuple(jnp.concatenate([a] * bb, axis=0) for a in tabs)

    sec_w = q_ref.shape[-1]
    for s, out_ref in enumerate((q_ref, k_ref, v_ref, xb_ref, gb_ref)):
        for h in range(sec_w // tn):
            col = s * sec_w + h * tn
            acc = jnp.dot(xn_scr[...], w_ref[:, col:col + tn], preferred_element_type=F32)
            if s == 0:
                acc = (_qk_norm_rope(acc, gq_ref[...], *tabs) * (QK_DIM ** -0.5)).astype(BF16)
            elif s == 1:
                acc = _qk_norm_rope(acc, gk_ref[...], *tabs)
            out_ref[:, :, h * tn:(h + 1) * tn] = acc.reshape(bb, tt, tn)


def _in_projection(x, mod, w_in, g_q, g_k, ropes, bb, tt, tn=512):
    b, t, d = x.shape
    n = w_in.shape[1]
    sec_w = n // 5
    rc, rs1, rs2 = ropes
    sec_spec = pl.BlockSpec((bb, tt, sec_w), lambda bi, i: (bi, i, 0))
    tab_spec = pl.BlockSpec((tt, LANES), lambda bi, i: (i, 0))
    g_spec = pl.BlockSpec((1, LANES), lambda bi, i: (0, 0))
    f32_out = jax.ShapeDtypeStruct((b, t, sec_w), F32)
    return pl.pallas_call(
        functools.partial(_inproj_body, bb=bb, tt=tt, tn=tn),
        grid=(b // bb, t // tt),
        in_specs=[pl.BlockSpec((bb, tt, d), lambda bi, i: (bi, i, 0)),
                  _mod_spec(bb, d, 0, 2), _mod_spec(bb, d, 1, 2),
                  pl.BlockSpec((d, n), lambda bi, i: (0, 0), pipeline_mode=pl.Buffered(1)),
                  g_spec, g_spec, tab_spec, tab_spec, tab_spec],
        out_specs=[sec_spec] * 5,
        out_shape=[jax.ShapeDtypeStruct((b, t, sec_w), BF16), f32_out, f32_out, f32_out, f32_out],
        scratch_shapes=[pltpu.VMEM((bb * tt, d), BF16)],
        compiler_params=_params(2),
        name="in_projection",
    )(x, mod, mod, w_in, g_q, g_k, rc, rs1, rs2)


def _gelu_tanh(x):
    return 0.5 * x * (1.0 + jnp.tanh(math.sqrt(2.0 / math.pi) * (x + 0.044715 * (x * x * x))))


def _lru_body(xb_ref, gb_ref, h0_ref, c0_ref, cw_ref, cb_ref, wax_ref, ba_ref, bx_ref, lam_ref,
              y_ref, hn_ref, cn_ref, ext_scr, h_scr, *, tt):
    i = pl.program_id(1)
    w = xb_ref.shape[-1]
    blk = w // N_LRU_BLOCKS
    groups = tt // SUBLANES

    @pl.when(i == 0)
    def _():
        ext_scr[0:SUBLANES, :] = c0_ref[0]
        h_scr[...] = h0_ref[0]

    ext_scr[SUBLANES:, :] = xb_ref[0]
    u = cb_ref[...] + cw_ref[0:1, :] * ext_scr[pl.ds(SUBLANES - 3, tt), :]
    for jj in range(1, CONV_W):
        u = u + cw_ref[jj:jj + 1, :] * ext_scr[pl.ds(SUBLANES - 3 + jj, tt), :]
    tail = ext_scr[tt:tt + SUBLANES, :]
    ext_scr[0:SUBLANES, :] = tail

    ub = u.astype(BF16)
    ga, gx = [], []
    for n in range(N_LRU_BLOCKS):
        res = jnp.dot(ub[:, n * blk:(n + 1) * blk], wax_ref[n], preferred_element_type=F32)
        ga.append(res[:, :blk])
        gx.append(res[:, blk:])
    r = jax.nn.sigmoid(jnp.concatenate(ga, axis=1) + ba_ref[...])
    ig = jax.nn.sigmoid(jnp.concatenate(gx, axis=1) + bx_ref[...])

    nl = -lam_ref[...]
    softplus = jnp.maximum(nl, 0.0) + jnp.log1p(jnp.exp(-jnp.abs(nl)))
    log_a = -LRU_C * r * softplus
    a = jnp.exp(log_a)
    b = jnp.sqrt(-jnp.tanh(log_a) * (a * a + 1.0)) * (ig * u)

    a3 = a.reshape(groups, SUBLANES, w)
    b3 = b.reshape(groups, SUBLANES, w)
    row = lax.broadcasted_iota(jnp.int32, (groups, SUBLANES, w), 1)
    for s in (1, 2, 4):
        keep = row >= s
        b3 = jnp.where(keep, a3 * pltpu.roll(b3, s, 1) + b3, b3)
        a3 = jnp.where(keep, a3 * pltpu.roll(a3, s, 1), a3)
    h_prev = h_scr[...]
    hs = []
    for g in range(groups):
        hg = a3[g] * h_prev + b3[g]
        hs.append(hg)
        h_prev = hg[SUBLANES - 1:SUBLANES, :]
    h = jnp.concatenate(hs, axis=0)
    h_scr[...] = h_prev

    y_ref[0] = (h * _gelu_tanh(gb_ref[0])).astype(BF16)

    @pl.when(i == pl.num_programs(1) - 1)
    def _():
        hn_ref[0] = h_prev
        cn_ref[0] = tail


def _rglru(xb, gb, h0, conv0, conv_w, conv_b, w_ax, b_a, b_x, lru_lambda, tt):
    b, t, w = xb.shape
    blk = w // N_LRU_BLOCKS
    c0 = jnp.pad(conv0, ((0, 0), (SUBLANES - (CONV_W - 1), 0), (0, 0)))
    row_spec = pl.BlockSpec((1, tt, w), lambda bi, i: (bi, i, 0))
    vec_spec = pl.BlockSpec((1, w), lambda bi, i: (0, 0))
    y, hn, cn = pl.pallas_call(
        functools.partial(_lru_body, tt=tt),
        grid=(b, t // tt),
        in_specs=[row_spec, row_spec,
                  pl.BlockSpec((1, 1, w), lambda bi, i: (bi, 0, 0)),
                  pl.BlockSpec((1, SUBLANES, w), lambda bi, i: (bi, 0, 0)),
                  pl.BlockSpec((CONV_W, w), lambda bi, i: (0, 0)),
                  vec_spec,
                  pl.BlockSpec((N_LRU_BLOCKS, blk, 2 * blk), lambda bi, i: (0, 0, 0)),
                  vec_spec, vec_spec, vec_spec],
        out_specs=[row_spec,
                   pl.BlockSpec((1, 1, w), lambda bi, i: (bi, 0, 0)),
                   pl.BlockSpec((1, SUBLANES, w), lambda bi, i: (bi, 0, 0))],
        out_shape=[jax.ShapeDtypeStruct((b, t, w), BF16),
                   jax.ShapeDtypeStruct((b, 1, w), F32),
                   jax.ShapeDtypeStruct((b, SUBLANES, w), F32)],
        scratch_shapes=[pltpu.VMEM((tt + SUBLANES, w), F32), pltpu.VMEM((1, w), F32)],
        compiler_params=_params(2),
        name="rglru",
    )(xb, gb, h0.reshape(b, 1, w), c0, conv_w, conv_b.reshape(1, w), w_ax,
      b_a.reshape(1, w), b_x.reshape(1, w), lru_lambda.reshape(1, w))
    return y, hn[:, 0], cn[:, SUBLANES - (CONV_W - 1):]


def _lambda_value(lq1_ref, lk1_ref, lq2_ref, lk2_ref):
    s1 = jnp.sum(lq1_ref[...] * lk1_ref[...], axis=-1, keepdims=True)
    s2 = jnp.sum(lq2_ref[...] * lk2_ref[...], axis=-1, keepdims=True)
    return jnp.exp(s1) - jnp.exp(s2) + LAM_INIT


def _stack_components(q):
    lane = lax.broadcasted_iota(jnp.int32, q.shape, 1)
    zero = jnp.zeros_like(q)
    return jnp.concatenate([jnp.where(lane < QK_DIM, q, zero),
                            jnp.where(lane < QK_DIM, zero, q)], axis=0)


def _attn_prompt_body(lq1_ref, lk1_ref, lq2_ref, lk2_ref, g_ref, q_ref, k_ref, v_ref, o_ref,
                      kb_scr, vt_scr, s_scr, *, tq):
    t = q_ref.shape[1]
    nq = t // tq
    kb_scr[...] = (k_ref[0] * LOG2E).astype(BF16)
    vt_scr[...] = v_ref[0].T.astype(BF16)
    lam = _lambda_value(lq1_ref, lk1_ref, lq2_ref, lk2_ref)
    gain = g_ref[...] * (1.0 - LAM_INIT)

    key_chunk = lax.broadcasted_iota(jnp.int32, (tq, 2 * tq), 0) // CHUNK
    qcol = lax.broadcasted_iota(jnp.int32, (tq, 2 * tq), 1)
    q_chunk = jnp.where(qcol >= tq, qcol - tq, qcol) // CHUNK
    visible = key_chunk <= q_chunk

    def fold(x):
        return x.reshape(tq // SUBLANES, SUBLANES, x.shape[-1])

    def stacked_q(i):
        return _stack_components(q_ref[0, i * tq:(i + 1) * tq, :])

    def score_chunk(i, c, qs, m8):
        sc = lax.dot_general(kb_scr[c * tq:(c + 1) * tq, :], qs, (((1,), (1,)), ((), ())),
                             preferred_element_type=F32)
        if c == i:
            sc = jnp.where(visible, sc, -1e30)
        s_scr[i % 2, c * tq:(c + 1) * tq, :] = sc
        return jnp.maximum(m8, jnp.max(fold(sc), axis=0))

    def value_chunk(i, c, m, l8, ot):
        p = jnp.exp2(s_scr[i % 2, c * tq:(c + 1) * tq, :] - m)
        ot = ot + jnp.dot(vt_scr[:, c * tq:(c + 1) * tq], p.astype(BF16),
                          preferred_element_type=F32)
        return l8 + jnp.sum(fold(p), axis=0), ot

    neg_inf = jnp.full((SUBLANES, 2 * tq), -jnp.inf, F32)
    qs = stacked_q(0)
    m8 = score_chunk(0, 0, qs, neg_inf)
    for i in range(nq):
        m = jnp.max(m8, axis=0, keepdims=True)
        l8 = jnp.zeros((SUBLANES, 2 * tq), F32)
        ot = jnp.zeros((V_DIM, 2 * tq), F32)
        if i + 1 < nq:
            qs, m8 = stacked_q(i + 1), neg_inf
        for c in range(i + 2):
            if c <= i:
                l8, ot = value_chunk(i, c, m, l8, ot)
            if i + 1 < nq:
                m8 = score_chunk(i + 1, c, qs, m8)
        ot = ot / jnp.sum(l8, axis=0, keepdims=True)
        o = ot[:, :tq] - lam * ot[:, tq:]
        ms = jnp.mean(o * o, axis=0, keepdims=True)
        o = o * lax.rsqrt(ms + EPS) * gain
        o_ref[0, i * tq:(i + 1) * tq, :] = o.T.astype(BF16)


def _attention_prompt(q, k, v, lams, g_subln, tq=256):
    b, t, aw = q.shape
    head = pl.BlockSpec((1, t, V_DIM), lambda bi, h: (bi, 0, h))
    lam_spec = pl.BlockSpec((1, QK_DIM), lambda bi, h: (0, 0))
    return pl.pallas_call(
        functools.partial(_attn_prompt_body, tq=tq),
        grid=(b, N_HEADS),
        in_specs=[lam_spec] * 4 + [pl.BlockSpec((V_DIM, 1), lambda bi, h: (0, 0)),
                                   head, head, head],
        out_specs=head,
        out_shape=jax.ShapeDtypeStruct((b, t, aw), BF16),
        scratch_shapes=[pltpu.VMEM((t, V_DIM), BF16), pltpu.VMEM((V_DIM, t), BF16),
                        pltpu.VMEM((2, t, 2 * tq), F32)],
        compiler_params=_params(2),
        name="attention_prompt",
    )(*lams, g_subln.reshape(V_DIM, 1), q, k, v)


def _attn_sample_body(lq1_ref, lk1_ref, lq2_ref, lk2_ref, g_ref, q_ref, kt_ref, pv_ref,
                      nk_ref, nv_ref, o_ref, *, tq):
    lam = _lambda_value(lq1_ref, lk1_ref, lq2_ref, lk2_ref)
    gain = g_ref[...] * (1.0 - LAM_INIT)
    for h in range(N_HEADS):
        cols = slice(h * V_DIM, (h + 1) * V_DIM)
        qs = _stack_components(q_ref[0, :, cols])
        s_past = jnp.dot(qs, kt_ref[0, cols, :].astype(BF16), preferred_element_type=F32)
        s_new = lax.dot_general(qs, nk_ref[0, :, cols].astype(BF16), (((1,), (1,)), ((), ())),
                                preferred_element_type=F32)
        m = jnp.maximum(jnp.max(s_past, axis=-1, keepdims=True),
                        jnp.max(s_new, axis=-1, keepdims=True))
        p_past = jnp.exp(s_past - m)
        p_new = jnp.exp(s_new - m)
        l = jnp.sum(p_past, axis=-1, keepdims=True) + jnp.sum(p_new, axis=-1, keepdims=True)
        o = jnp.dot(p_past.astype(BF16), pv_ref[:, h, :].astype(BF16),
                    preferred_element_type=F32)
        o = o + jnp.dot(p_new.astype(BF16), nv_ref[0, :, cols].astype(BF16),
                        preferred_element_type=F32)
        o = o / l
        o = o[:tq] - lam * o[tq:]
        ms = jnp.mean(o * o, axis=-1, keepdims=True)
        o_ref[0, :, cols] = (o * lax.rsqrt(ms + EPS) * gain).astype(BF16)


def _attention_sample(q, past_kt, past_v, new_k, new_v, lams, g_subln):
    b, t, aw = q.shape
    past = past_kt.shape[-1]
    assert past % CHUNK == 0 and t <= CHUNK
    row_new = pl.BlockSpec((1, t, aw), lambda bi: (bi, 0, 0))
    vec = lambda width: pl.BlockSpec((1, width), lambda bi: (0, 0))
    return pl.pallas_call(
        functools.partial(_attn_sample_body, tq=t),
        grid=(b,),
        in_specs=[vec(QK_DIM)] * 4 + [
            vec(V_DIM), row_new,
            pl.BlockSpec((1, aw, past), lambda bi: (bi, 0, 0)),
            pl.BlockSpec((None, past, N_HEADS, V_DIM), lambda bi: (bi, 0, 0, 0)),
            row_new, row_new],
        out_specs=row_new,
        out_shape=jax.ShapeDtypeStruct((b, t, aw), BF16),
        compiler_params=_params(1),
        name="attention_sample",
    )(*lams, g_subln, q, past_kt, past_v, new_k, new_v)


def _mlp_body(x_ref, o_ref, y_ref, gt1_ref, sh2_ref, sc2_ref, gt2_ref, wo_ref, wu_ref, wd_ref,
              out_ref, xn_scr, hid_scr, acc_scr, *, bb, tt):
    f = pl.program_id(2)
    last = pl.num_programs(2) - 1
    m = bb * tt
    d = x_ref.shape[-1]
    aw = o_ref.shape[-1]

    @pl.when(f == 0)
    def _():
        mix = jnp.dot(o_ref[...].reshape(m, aw), wo_ref[0:aw, :], preferred_element_type=F32)
        mix = mix + jnp.dot(y_ref[...].reshape(m, y_ref.shape[-1]), wo_ref[aw:, :],
                            preferred_element_type=F32)
        x1 = x_ref[...] + gt1_ref[...] * mix.reshape(bb, tt, d)
        out_ref[...] = x1
        ms = jnp.mean(x1 * x1, axis=-1, keepdims=True)
        xn = x1 * lax.rsqrt(ms + EPS) * (1.0 + sc2_ref[...]) + sh2_ref[...]
        xn_scr[...] = xn.reshape(m, d).astype(BF16)
        acc_scr[...] = jnp.zeros(acc_scr.shape, F32)

    def up(slot):
        hid = jnp.maximum(jnp.dot(xn_scr[...], wu_ref[...], preferred_element_type=F32), 0.0)
        hid_scr[slot] = (hid * hid).astype(BF16)

    def down(slot):
        acc_scr[...] += jnp.dot(hid_scr[slot], wd_ref[...], preferred_element_type=F32)

    @pl.when(f == 0)
    def _():
        up(0)

    @pl.when(jnp.logical_and(f > 0, f < last))
    def _():
        down((f - 1) % 2)
        up(f % 2)

    @pl.when(f == last)
    def _():
        down((f - 1) % 2)
        out_ref[...] = out_ref[...] + gt2_ref[...] * acc_scr[...].reshape(bb, tt, d)


def _outproj_mlp(x, o, y, mod, w_out, w_up, w_down, bb, tt, tf=1024):
    b, t, d = x.shape
    aw, lw = o.shape[-1], y.shape[-1]
    nf = w_up.shape[1] // tf
    row = lambda width: pl.BlockSpec((bb, tt, width), lambda bi, i, f: (bi, i, 0))
    return pl.pallas_call(
        functools.partial(_mlp_body, bb=bb, tt=tt),
        grid=(b // bb, t // tt, nf + 1),
        in_specs=[row(d), row(aw), row(lw),
                  _mod_spec(bb, d, 2, 3), _mod_spec(bb, d, 3, 3),
                  _mod_spec(bb, d, 4, 3), _mod_spec(bb, d, 5, 3),
                  pl.BlockSpec((aw + lw, d), lambda bi, i, f: (0, 0),
                               pipeline_mode=pl.Buffered(1)),
                  pl.BlockSpec((d, tf), lambda bi, i, f: (0, jnp.minimum(f, nf - 1))),
                  pl.BlockSpec((tf, d), lambda bi, i, f: (jnp.maximum(f - 1, 0), 0))],
        out_specs=row(d),
        out_shape=jax.ShapeDtypeStruct((b, t, d), F32),
        scratch_shapes=[pltpu.VMEM((bb * tt, d), BF16), pltpu.VMEM((2, bb * tt, tf), BF16),
                        pltpu.VMEM((bb * tt, d), F32)],
        compiler_params=_params(3),
        name="outproj_mlp",
    )(x, o, y, mod, mod, mod, mod, w_out, w_up, w_down)


def _layer_common(x, mod, pos, h0, conv0, weights, bb, tt, lru_tt):
    (w_in, g_q, g_k, conv_w, conv_b, w_ax, b_a, b_x, lru_lambda) = weights
    q, k, v, xb, gb = _in_projection(x, mod, w_in, g_q, g_k, _rope_tables(pos), bb, tt)
    y, h_new, conv_new = _rglru(xb, gb, h0, conv0, conv_w, conv_b, w_ax, b_a, b_x, lru_lambda,
                                lru_tt)
    return q, k, v, y, h_new, conv_new


def kernel(x_prompt, x_sample, c_prompt, c_sample, cache_k, cache_v, state_lru_h, state_conv,
           w_ada, b_ada, w_in, g_q, g_k, lambda_q1, lambda_k1, lambda_q2, lambda_k2, g_subln,
           conv_w, conv_b, w_gate_a, b_gate_a, w_gate_x, b_gate_x, lru_lambda,
           w_out, w_up, w_down):
    bp, tp, d = x_prompt.shape
    bs, ts, _ = x_sample.shape
    past = cache_k.shape[2]
    aw = N_HEADS * V_DIM
    lw = conv_w.shape[-1]

    c_all = jnp.concatenate([c_prompt, c_sample], axis=0)
    mod = _modulation(c_all, w_ada[0], b_ada[0]).reshape(bp + bs, N_MOD, 1, d)
    mod_p, mod_s = mod[:bp], mod[bp:]

    rep = LANES // QK_DIM
    weights = (w_in[0].astype(BF16),
               jnp.tile(g_q[0], rep).reshape(1, LANES), jnp.tile(g_k[0], rep).reshape(1, LANES),
               conv_w[0], conv_b[0],
               jnp.concatenate([w_gate_a[0], w_gate_x[0]], axis=-1).astype(BF16),
               b_gate_a[0].reshape(-1), b_gate_x[0].reshape(-1), lru_lambda[0])
    lams = tuple(a[0].reshape(1, QK_DIM) for a in (lambda_q1, lambda_k1, lambda_q2, lambda_k2))
    g_sub = g_subln[0].reshape(1, V_DIM)
    wo, wu, wd = w_out[0].astype(BF16), w_up[0].astype(BF16), w_down[0].astype(BF16)

    tt_p = min(tp, 512)
    qp, kp, vp, yp, hp, cp = _layer_common(
        x_prompt, mod_p, jnp.arange(tp), jnp.zeros((bp, lw), F32),
        jnp.zeros((bp, CONV_W - 1, lw), F32), weights, 1, tt_p, min(tp, 256))
    op = _attention_prompt(qp, kp, vp, lams, g_sub, tq=min(tp, 256))
    out_p = _outproj_mlp(x_prompt, op, yp, mod_p, wo, wu, wd, 1, tt_p)

    qs, ks, vs, ys, hs, cs = _layer_common(
        x_sample, mod_s, past + jnp.arange(ts), state_lru_h[0], state_conv[0], weights,
        bs, ts, ts)
    past_kt = jnp.transpose(cache_k[0], (0, 2, 3, 4, 1)).reshape(bs, aw, past)
    osamp = _attention_sample(qs, past_kt, cache_v[0], ks, vs, lams, g_sub)
    out_s = _outproj_mlp(x_sample, osamp, ys, mod_s, wo, wu, wd, bs, ts)

    return (out_p, out_s,
            kp.reshape(1, bp, tp, N_HEADS, 2, QK_DIM), vp.reshape(1, bp, tp, N_HEADS, V_DIM),
            hp[None], cp[None],
            ks.reshape(1, bs, ts, N_HEADS, 2, QK_DIM), vs.reshape(1, bs, ts, N_HEADS, V_DIM),
            hs[None], cs[None])
```

```python
import functools
import math

import jax
import jax.numpy as jnp
from jax import lax
from jax.experimental import pallas as pl
from jax.experimental.pallas import tpu as pltpu

F32 = jnp.float32
BF16 = jnp.bfloat16

LANES = 128
SUBLANES = 8
VMEM_LIMIT = 56 * 1024 * 1024

N_HEADS = 8
QK_DIM = 64
V_DIM = 2 * QK_DIM
CHUNK = 64
ROT_DIM = QK_DIM // 4
ROPE_THETA = 500000.0
N_LRU_BLOCKS = 8
CONV_W = 4
LRU_C = 8.0
N_MOD = 6
EPS = 1e-6
LAM_INIT = 0.8 - 0.6 * math.exp(-0.3 * 0)
LOG2E = math.log2(math.e)


def _params(n_axes):
    return pltpu.CompilerParams(dimension_semantics=("arbitrary",) * n_axes,
                                vmem_limit_bytes=VMEM_LIMIT)


def _mod_body(c_ref, w_ref, b_ref, o_ref):
    c = c_ref[...]
    s = c * jax.nn.sigmoid(c)
    o_ref[...] = jnp.dot(s.astype(BF16), w_ref[...].astype(BF16),
                         preferred_element_type=F32) + b_ref[...]


def _modulation(c, w_ada, b_ada, tn=1024):
    rows, d = c.shape
    n = w_ada.shape[1]
    return pl.pallas_call(
        _mod_body,
        grid=(n // tn,),
        in_specs=[pl.BlockSpec((rows, d), lambda j: (0, 0)),
                  pl.BlockSpec((d, tn), lambda j: (0, j)),
                  pl.BlockSpec((1, tn), lambda j: (0, j))],
        out_specs=pl.BlockSpec((rows, tn), lambda j: (0, j)),
        out_shape=jax.ShapeDtypeStruct((rows, n), F32),
        compiler_params=_params(1),
        name="modulation",
    )(c, w_ada, b_ada.reshape(1, n))


def _mod_spec(bb, d, which, n_axes):
    if n_axes == 3:
        return pl.BlockSpec((bb, None, 1, d), lambda b, i, j: (b, which, 0, 0))
    return pl.BlockSpec((bb, None, 1, d), lambda b, i: (b, which, 0, 0))


def _rope_tables(pos):
    half = ROT_DIM // 2
    inv_freq = ROPE_THETA ** (-(jnp.arange(half, dtype=F32) * 2.0) / ROT_DIM)
    ang = pos.astype(F32)[:, None] * inv_freq[None, :]
    cos, sin = jnp.cos(ang), jnp.sin(ang)
    t = pos.shape[0]
    ones = jnp.ones((t, QK_DIM - ROT_DIM), F32)
    zeros_h = jnp.zeros((t, half), F32)
    zeros_r = jnp.zeros((t, QK_DIM - ROT_DIM), F32)
    c = jnp.concatenate([cos, cos, ones], axis=1)
    s1 = jnp.concatenate([zeros_h, sin, zeros_r], axis=1)
    s2 = jnp.concatenate([-sin, zeros_h, zeros_r], axis=1)
    rep = LANES // QK_DIM
    return tuple(jnp.tile(a, (1, rep)) for a in (c, s1, s2))


def _qk_norm_rope(acc, g, c, s1, s2):
    m, tn = acc.shape
    lane = lax.broadcasted_iota(jnp.int32, (m, LANES), 1)
    lo = lane < QK_DIM
    outs = []
    for t in range(tn // LANES):
        z = acc[:, t * LANES:(t + 1) * LANES]
        zz = z * z
        s_lo = jnp.sum(jnp.where(lo, zz, 0.0), axis=-1, keepdims=True)
        s_hi = jnp.sum(jnp.where(lo, 0.0, zz), axis=-1, keepdims=True)
        r = jnp.where(lo, lax.rsqrt(s_lo * (1.0 / QK_DIM) + EPS),
                      lax.rsqrt(s_hi * (1.0 / QK_DIM) + EPS))
        y = z * r * g
        y = (y * c + pltpu.roll(y, ROT_DIM // 2, 1) * s1
             + pltpu.roll(y, LANES - ROT_DIM // 2, 1) * s2)
        outs.append(y)
    return jnp.concatenate(outs, axis=1)


def _gelu_tanh(x):
    return 0.5 * x * (1.0 + jnp.tanh(math.sqrt(2.0 / math.pi) * (x + 0.044715 * (x * x * x))))


def _lru_tile(ext, gb, h_prev, cw, cb, wax_ref, ba, bx, lam):
    tt = ext.shape[0] - SUBLANES
    w = ext.shape[1]
    n_blocks = wax_ref.shape[0]
    blk = w // n_blocks
    groups = tt // SUBLANES
    first = SUBLANES - (CONV_W - 1)
    u = cb + cw[0:1, :] * ext[pl.ds(first, tt), :]
    for jj in range(1, CONV_W):
        u = u + cw[jj:jj + 1, :] * ext[pl.ds(first + jj, tt), :]

    ub = u.astype(BF16)
    ga, gx = [], []
    for n in range(n_blocks):
        res = jnp.dot(ub[:, n * blk:(n + 1) * blk], wax_ref[n], preferred_element_type=F32)
        ga.append(res[:, :blk])
        gx.append(res[:, blk:])
    r = jax.nn.sigmoid(jnp.concatenate(ga, axis=1) + ba)
    ig = jax.nn.sigmoid(jnp.concatenate(gx, axis=1) + bx)

    nl = -lam
    softplus = jnp.maximum(nl, 0.0) + jnp.log1p(jnp.exp(-jnp.abs(nl)))
    log_a = -LRU_C * r * softplus
    a = jnp.exp(log_a)
    b = jnp.sqrt(-jnp.tanh(log_a) * (a * a + 1.0)) * (ig * u)

    a3 = a.reshape(groups, SUBLANES, w)
    b3 = b.reshape(groups, SUBLANES, w)
    row = lax.broadcasted_iota(jnp.int32, (groups, SUBLANES, w), 1)
    for s in (1, 2, 4):
        keep = row >= s
        b3 = jnp.where(keep, a3 * pltpu.roll(b3, s, 1) + b3, b3)
        a3 = jnp.where(keep, a3 * pltpu.roll(a3, s, 1), a3)
    hs = []
    for g in range(groups):
        hg = a3[g] * h_prev + b3[g]
        hs.append(hg)
        h_prev = hg[SUBLANES - 1:SUBLANES, :]
    h = jnp.concatenate(hs, axis=0)
    return h * _gelu_tanh(gb), h_prev


def _inproj_body(x_ref, sh_ref, sc_ref, w_ref, gq_ref, gk_ref, rc_ref, rs1_ref, rs2_ref,
                 h0_ref, c0_ref, cw_ref, cb_ref, wax_ref, ba_ref, bx_ref, lam_ref,
                 q_ref, k_ref, v_ref, y_ref, hn_ref, cn_ref,
                 xn_scr, ext_scr, gb_scr, h_scr, *, bb, tt, tn):
    i = pl.program_id(1)
    m = bb * tt
    d = x_ref.shape[-1]
    x = x_ref[...]
    ms = jnp.mean(x * x, axis=-1, keepdims=True)
    xn = x * lax.rsqrt(ms + EPS) * (1.0 + sc_ref[...]) + sh_ref[...]
    xn_scr[...] = xn.reshape(m, d).astype(BF16)

    @pl.when(i == 0)
    def _():
        ext_scr[:, 0:SUBLANES, :] = c0_ref[...]
        h_scr[...] = h0_ref[...]

    sec_w = q_ref.shape[-1]
    tiles = sec_w // tn

    def project(s, h):
        col = s * sec_w + h * tn
        return jnp.dot(xn_scr[...], w_ref[:, col:col + tn], preferred_element_type=F32)

    for h in range(tiles):
        ext_scr[:, SUBLANES:, h * tn:(h + 1) * tn] = project(3, h).reshape(bb, tt, tn)
    for h in range(tiles):
        gb_scr[:, h * tn:(h + 1) * tn] = project(4, h)
    for b in range(bb):
        y, h_last = _lru_tile(ext_scr.at[b], gb_scr[b * tt:(b + 1) * tt, :], h_scr[b],
                              cw_ref[...], cb_ref[...], wax_ref, ba_ref[...], bx_ref[...],
                              lam_ref[...])
        y_ref[b] = y.astype(BF16)
        h_scr[b] = h_last
    ext_scr[:, 0:SUBLANES, :] = ext_scr[:, tt:tt + SUBLANES, :]
    hn_ref[...] = h_scr[...]
    cn_ref[...] = ext_scr[:, 0:SUBLANES, :]

    tabs = (rc_ref[...], rs1_ref[...], rs2_ref[...])
    if bb > 1:
        tabs = tuple(jnp.concatenate([a] * bb, axis=0) for a in tabs)
    for h in range(tiles):
        acc = _qk_norm_rope(project(0, h), gq_ref[...], *tabs) * (QK_DIM ** -0.5)
        q_ref[:, :, h * tn:(h + 1) * tn] = acc.astype(BF16).reshape(bb, tt, tn)
    for h in range(tiles):
        acc = _qk_norm_rope(project(1, h), gk_ref[...], *tabs)
        k_ref[:, :, h * tn:(h + 1) * tn] = acc.reshape(bb, tt, tn)
    for h in range(tiles):
        v_ref[:, :, h * tn:(h + 1) * tn] = project(2, h).reshape(bb, tt, tn)


def _in_projection(x, mod, w_in, g_q, g_k, ropes, h0, conv0, lru_weights, bb, tt, tn=512):
    b, t, d = x.shape
    n = w_in.shape[1]
    sec_w = n // 5
    rc, rs1, rs2 = ropes
    conv_w, conv_b, w_ax, b_a, b_x, lru_lambda = lru_weights
    w = conv_w.shape[-1]
    blk = w // N_LRU_BLOCKS
    c0 = jnp.pad(conv0, ((0, 0), (SUBLANES - (CONV_W - 1), 0), (0, 0)))
    sec_spec = pl.BlockSpec((bb, tt, sec_w), lambda bi, i: (bi, i, 0))
    tab_spec = pl.BlockSpec((tt, LANES), lambda bi, i: (i, 0))
    g_spec = pl.BlockSpec((1, LANES), lambda bi, i: (0, 0))
    state_spec = lambda rows: pl.BlockSpec((bb, rows, w), lambda bi, i: (bi, 0, 0))
    vec_spec = lambda rows: pl.BlockSpec((rows, w), lambda bi, i: (0, 0))
    f32_out = jax.ShapeDtypeStruct((b, t, sec_w), F32)
    q, k, v, y, hn, cn = pl.pallas_call(
        functools.partial(_inproj_body, bb=bb, tt=tt, tn=tn),
        grid=(b // bb, t // tt),
        in_specs=[pl.BlockSpec((bb, tt, d), lambda bi, i: (bi, i, 0)),
                  _mod_spec(bb, d, 0, 2), _mod_spec(bb, d, 1, 2),
                  pl.BlockSpec((d, n), lambda bi, i: (0, 0), pipeline_mode=pl.Buffered(1)),
                  g_spec, g_spec, tab_spec, tab_spec, tab_spec,
                  state_spec(1), state_spec(SUBLANES), vec_spec(CONV_W), vec_spec(1),
                  pl.BlockSpec((N_LRU_BLOCKS, blk, 2 * blk), lambda bi, i: (0, 0, 0)),
                  vec_spec(1), vec_spec(1), vec_spec(1)],
        out_specs=[sec_spec, sec_spec, sec_spec, sec_spec, state_spec(1), state_spec(SUBLANES)],
        out_shape=[jax.ShapeDtypeStruct((b, t, sec_w), BF16), f32_out, f32_out,
                   jax.ShapeDtypeStruct((b, t, w), BF16),
                   jax.ShapeDtypeStruct((b, 1, w), F32),
                   jax.ShapeDtypeStruct((b, SUBLANES, w), F32)],
        scratch_shapes=[pltpu.VMEM((bb * tt, d), BF16),
                        pltpu.VMEM((bb, tt + SUBLANES, w), F32),
                        pltpu.VMEM((bb * tt, w), F32),
                        pltpu.VMEM((bb, 1, w), F32)],
        compiler_params=_params(2),
        name="in_projection",
    )(x, mod, mod, w_in, g_q, g_k, rc, rs1, rs2, h0.reshape(b, 1, w), c0,
      conv_w, conv_b.reshape(1, w), w_ax, b_a.reshape(1, w), b_x.reshape(1, w),
      lru_lambda.reshape(1, w))
    return q, k, v, y, hn[:, 0], cn[:, SUBLANES - (CONV_W - 1):]


def _lambda_value(lq1_ref, lk1_ref, lq2_ref, lk2_ref):
    s1 = jnp.sum(lq1_ref[...] * lk1_ref[...], axis=-1, keepdims=True)
    s2 = jnp.sum(lq2_ref[...] * lk2_ref[...], axis=-1, keepdims=True)
    return jnp.exp(s1) - jnp.exp(s2) + LAM_INIT


def _stack_components(q):
    lane = lax.broadcasted_iota(jnp.int32, q.shape, 1)
    zero = jnp.zeros_like(q)
    return jnp.concatenate([jnp.where(lane < QK_DIM, q, zero),
                            jnp.where(lane < QK_DIM, zero, q)], axis=0)


def _attn_prompt_body(lq1_ref, lk1_ref, lq2_ref, lk2_ref, g_ref, q_ref, k_ref, v_ref, o_ref,
                      kb_scr, vt_scr, s_scr, *, tq):
    t = q_ref.shape[1]
    nq = t // tq
    kb_scr[...] = (k_ref[0] * LOG2E).astype(BF16)
    vt_scr[...] = v_ref[0].T.astype(BF16)
    lam = _lambda_value(lq1_ref, lk1_ref, lq2_ref, lk2_ref)
    gain = g_ref[...] * (1.0 - LAM_INIT)

    key_chunk = lax.broadcasted_iota(jnp.int32, (tq, 2 * tq), 0) // CHUNK
    qcol = lax.broadcasted_iota(jnp.int32, (tq, 2 * tq), 1)
    q_chunk = jnp.where(qcol >= tq, qcol - tq, qcol) // CHUNK
    visible = key_chunk <= q_chunk

    def fold(x):
        return x.reshape(tq // SUBLANES, SUBLANES, x.shape[-1])

    def stacked_q(i):
        return _stack_components(q_ref[0, i * tq:(i + 1) * tq, :])

    def score_chunk(i, c, qs, m8):
        sc = lax.dot_general(kb_scr[c * tq:(c + 1) * tq, :], qs, (((1,), (1,)), ((), ())),
                             preferred_element_type=F32)
        if c == i:
            sc = jnp.where(visible, sc, -1e30)
        s_scr[i % 2, c * tq:(c + 1) * tq, :] = sc
        return jnp.maximum(m8, jnp.max(fold(sc), axis=0))

    def value_chunk(i, c, m, l8, ot):
        p = jnp.exp2(s_scr[i % 2, c * tq:(c + 1) * tq, :] - m)
        ot = ot + jnp.dot(vt_scr[:, c * tq:(c + 1) * tq], p.astype(BF16),
                          preferred_element_type=F32)
        return l8 + jnp.sum(fold(p), axis=0), ot

    neg_inf = jnp.full((SUBLANES, 2 * tq), -jnp.inf, F32)
    qs = stacked_q(0)
    m8 = score_chunk(0, 0, qs, neg_inf)
    for i in range(nq):
        m = jnp.max(m8, axis=0, keepdims=True)
        l8 = jnp.zeros((SUBLANES, 2 * tq), F32)
        ot = jnp.zeros((V_DIM, 2 * tq), F32)
        if i + 1 < nq:
            qs, m8 = stacked_q(i + 1), neg_inf
        for c in range(i + 2):
            if c <= i:
                l8, ot = value_chunk(i, c, m, l8, ot)
            if i + 1 < nq:
                m8 = score_chunk(i + 1, c, qs, m8)
        ot = ot / jnp.sum(l8, axis=0, keepdims=True)
        o = ot[:, :tq] - lam * ot[:, tq:]
        ms = jnp.mean(o * o, axis=0, keepdims=True)
        o = o * lax.rsqrt(ms + EPS) * gain
        o_ref[0, i * tq:(i + 1) * tq, :] = o.T.astype(BF16)


def _attention_prompt(q, k, v, lams, g_subln, tq=256):
    b, t, aw = q.shape
    head = pl.BlockSpec((1, t, V_DIM), lambda bi, h: (bi, 0, h))
    lam_spec = pl.BlockSpec((1, QK_DIM), lambda bi, h: (0, 0))
    return pl.pallas_call(
        functools.partial(_attn_prompt_body, tq=tq),
        grid=(b, N_HEADS),
        in_specs=[lam_spec] * 4 + [pl.BlockSpec((V_DIM, 1), lambda bi, h: (0, 0)),
                                   head, head, head],
        out_specs=head,
        out_shape=jax.ShapeDtypeStruct((b, t, aw), BF16),
        scratch_shapes=[pltpu.VMEM((t, V_DIM), BF16), pltpu.VMEM((V_DIM, t), BF16),
                        pltpu.VMEM((2, t, 2 * tq), F32)],
        compiler_params=_params(2),
        name="attention_prompt",
    )(*lams, g_subln.reshape(V_DIM, 1), q, k, v)


def _attn_sample_body(lq1_ref, lk1_ref, lq2_ref, lk2_ref, g_ref, q_ref, kt_ref, pv_ref,
                      nk_ref, nv_ref, o_ref, *, tq):
    lam = _lambda_value(lq1_ref, lk1_ref, lq2_ref, lk2_ref)
    gain = g_ref[...] * (1.0 - LAM_INIT)
    for h in range(N_HEADS):
        cols = slice(h * V_DIM, (h + 1) * V_DIM)
        qs = _stack_components(q_ref[0, :, cols])
        s_past = jnp.dot(qs, kt_ref[0, cols, :].astype(BF16), preferred_element_type=F32)
        s_new = lax.dot_general(qs, nk_ref[0, :, cols].astype(BF16), (((1,), (1,)), ((), ())),
                                preferred_element_type=F32)
        m = jnp.maximum(jnp.max(s_past, axis=-1, keepdims=True),
                        jnp.max(s_new, axis=-1, keepdims=True))
        p_past = jnp.exp(s_past - m)
        p_new = jnp.exp(s_new - m)
        l = jnp.sum(p_past, axis=-1, keepdims=True) + jnp.sum(p_new, axis=-1, keepdims=True)
        o = jnp.dot(p_past.astype(BF16), pv_ref[:, h, :].astype(BF16),
                    preferred_element_type=F32)
        o = o + jnp.dot(p_new.astype(BF16), nv_ref[0, :, cols].astype(BF16),
                        preferred_element_type=F32)
        o = o / l
        o = o[:tq] - lam * o[tq:]
        ms = jnp.mean(o * o, axis=-1, keepdims=True)
        o_ref[0, :, cols] = (o * lax.rsqrt(ms + EPS) * gain).astype(BF16)


def _attention_sample(q, past_kt, past_v, new_k, new_v, lams, g_subln):
    b, t, aw = q.shape
    past = past_kt.shape[-1]
    assert past % CHUNK == 0 and t <= CHUNK
    row_new = pl.BlockSpec((1, t, aw), lambda bi: (bi, 0, 0))
    vec = lambda width: pl.BlockSpec((1, width), lambda bi: (0, 0))
    return pl.pallas_call(
        functools.partial(_attn_sample_body, tq=t),
        grid=(b,),
        in_specs=[vec(QK_DIM)] * 4 + [
            vec(V_DIM), row_new,
            pl.BlockSpec((1, aw, past), lambda bi: (bi, 0, 0)),
            pl.BlockSpec((None, past, N_HEADS, V_DIM), lambda bi: (bi, 0, 0, 0)),
            row_new, row_new],
        out_specs=row_new,
        out_shape=jax.ShapeDtypeStruct((b, t, aw), BF16),
        compiler_params=_params(1),
        name="attention_sample",
    )(*lams, g_subln, q, past_kt, past_v, new_k, new_v)


def _mlp_body(x_ref, o_ref, y_ref, gt1_ref, sh2_ref, sc2_ref, gt2_ref, wo_ref, wu_ref, wd_ref,
              out_ref, xn_scr, acc_scr, *, bb, tt):
    f = pl.program_id(2)
    m = bb * tt
    d = x_ref.shape[-1]
    aw = o_ref.shape[-1]

    @pl.when(f == 0)
    def _():
        mix = jnp.dot(o_ref[...].reshape(m, aw), wo_ref[0:aw, :], preferred_element_type=F32)
        mix = mix + jnp.dot(y_ref[...].reshape(m, y_ref.shape[-1]), wo_ref[aw:, :],
                            preferred_element_type=F32)
        x1 = x_ref[...] + gt1_ref[...] * mix.reshape(bb, tt, d)
        out_ref[...] = x1
        ms = jnp.mean(x1 * x1, axis=-1, keepdims=True)
        xn = x1 * lax.rsqrt(ms + EPS) * (1.0 + sc2_ref[...]) + sh2_ref[...]
        xn_scr[...] = xn.reshape(m, d).astype(BF16)
        acc_scr[...] = jnp.zeros(acc_scr.shape, F32)

    hid = jnp.maximum(jnp.dot(xn_scr[...], wu_ref[...], preferred_element_type=F32), 0.0)
    acc_scr[...] += jnp.dot((hid * hid).astype(BF16), wd_ref[...], preferred_element_type=F32)

    @pl.when(f == pl.num_programs(2) - 1)
    def _():
        out_ref[...] = out_ref[...] + gt2_ref[...] * acc_scr[...].reshape(bb, tt, d)


def _outproj_mlp(x, o, y, mod, w_out, w_up, w_down, bb, tt, tf=1024):
    b, t, d = x.shape
    aw, lw = o.shape[-1], y.shape[-1]
    dff = w_up.shape[1]
    row = lambda width: pl.BlockSpec((bb, tt, width), lambda bi, i, f: (bi, i, 0))
    return pl.pallas_call(
        functools.partial(_mlp_body, bb=bb, tt=tt),
        grid=(b // bb, t // tt, dff // tf),
        in_specs=[row(d), row(aw), row(lw),
                  _mod_spec(bb, d, 2, 3), _mod_spec(bb, d, 3, 3),
                  _mod_spec(bb, d, 4, 3), _mod_spec(bb, d, 5, 3),
                  pl.BlockSpec((aw + lw, d), lambda bi, i, f: (0, 0),
                               pipeline_mode=pl.Buffered(1)),
                  pl.BlockSpec((d, tf), lambda bi, i, f: (0, f)),
                  pl.BlockSpec((tf, d), lambda bi, i, f: (f, 0))],
        out_specs=row(d),
        out_shape=jax.ShapeDtypeStruct((b, t, d), F32),
        scratch_shapes=[pltpu.VMEM((bb * tt, d), BF16), pltpu.VMEM((bb * tt, d), F32)],
        compiler_params=_params(3),
        name="outproj_mlp",
    )(x, o, y, mod, mod, mod, mod, w_out, w_up, w_down)


def _layer_common(x, mod, pos, h0, conv0, weights, bb, tt):
    (w_in, g_q, g_k, *lru_weights) = weights
    return _in_projection(x, mod, w_in, g_q, g_k, _rope_tables(pos), h0, conv0, lru_weights,
                          bb, tt)


def kernel(x_prompt, x_sample, c_prompt, c_sample, cache_k, cache_v, state_lru_h, state_conv,
           w_ada, b_ada, w_in, g_q, g_k, lambda_q1, lambda_k1, lambda_q2, lambda_k2, g_subln,
           conv_w, conv_b, w_gate_a, b_gate_a, w_gate_x, b_gate_x, lru_lambda,
           w_out, w_up, w_down):
    bp, tp, d = x_prompt.shape
    bs, ts, _ = x_sample.shape
    past = cache_k.shape[2]
    aw = N_HEADS * V_DIM
    lw = conv_w.shape[-1]

    c_all = jnp.concatenate([c_prompt, c_sample], axis=0)
    mod = _modulation(c_all, w_ada[0], b_ada[0]).reshape(bp + bs, N_MOD, 1, d)
    mod_p, mod_s = mod[:bp], mod[bp:]

    rep = LANES // QK_DIM
    weights = (w_in[0].astype(BF16),
               jnp.tile(g_q[0], rep).reshape(1, LANES), jnp.tile(g_k[0], rep).reshape(1, LANES),
               conv_w[0], conv_b[0],
               jnp.concatenate([w_gate_a[0], w_gate_x[0]], axis=-1).astype(BF16),
               b_gate_a[0].reshape(-1), b_gate_x[0].reshape(-1), lru_lambda[0])
    lams = tuple(a[0].reshape(1, QK_DIM) for a in (lambda_q1, lambda_k1, lambda_q2, lambda_k2))
    g_sub = g_subln[0].reshape(1, V_DIM)
    wo, wu, wd = w_out[0].astype(BF16), w_up[0].astype(BF16), w_down[0].astype(BF16)

    tt_p = min(tp, 512)
    qp, kp, vp, yp, hp, cp = _layer_common(
        x_prompt, mod_p, jnp.arange(tp), jnp.zeros((bp, lw), F32),
        jnp.zeros((bp, CONV_W - 1, lw), F32), weights, 1, tt_p)
    op = _attention_prompt(qp, kp, vp, lams, g_sub, tq=min(tp, 256))
    out_p = _outproj_mlp(x_prompt, op, yp, mod_p, wo, wu, wd, 1, tt_p)

    qs, ks, vs, ys, hs, cs = _layer_common(
        x_sample, mod_s, past + jnp.arange(ts), state_lru_h[0], state_conv[0], weights,
        bs, ts)
    past_kt = jnp.transpose(cache_k[0], (0, 2, 3, 4, 1)).reshape(bs, aw, past)
    osamp = _attention_sample(qs, past_kt, cache_v[0], ks, vs, lams, g_sub)
    out_s = _outproj_mlp(x_sample, osamp, ys, mod_s, wo, wu, wd, bs, ts)

    return (out_p, out_s,
            kp.reshape(1, bp, tp, N_HEADS, 2, QK_DIM), vp.reshape(1, bp, tp, N_HEADS, V_DIM),
            hp[None], cp[None],
            ks.reshape(1, bs, ts, N_HEADS, 2, QK_DIM), vs.reshape(1, bs, ts, N_HEADS, V_DIM),
            hs[None], cs[None])
```

```python
import functools
import math

import jax
import jax.numpy as jnp
from jax import lax
from jax.experimental import pallas as pl
from jax.experimental.pallas import tpu as pltpu

F32 = jnp.float32
BF16 = jnp.bfloat16

LANES = 128
SUBLANES = 8
VMEM_LIMIT = 56 * 1024 * 1024

N_HEADS = 8
QK_DIM = 64
V_DIM = 2 * QK_DIM
CHUNK = 64
ROT_DIM = QK_DIM // 4
ROPE_THETA = 500000.0
N_LRU_BLOCKS = 8
CONV_W = 4
LRU_C = 8.0
N_MOD = 6
EPS = 1e-6
LAM_INIT = 0.8 - 0.6 * math.exp(-0.3 * 0)
LOG2E = math.log2(math.e)


def _params(n_axes):
    return pltpu.CompilerParams(dimension_semantics=("arbitrary",) * n_axes,
                                vmem_limit_bytes=VMEM_LIMIT)


def _mod_body(c_ref, w_ref, b_ref, o_ref):
    c = c_ref[...]
    s = c * jax.nn.sigmoid(c)
    o_ref[...] = jnp.dot(s.astype(BF16), w_ref[...].astype(BF16),
                         preferred_element_type=F32) + b_ref[...]


def _modulation(c, w_ada, b_ada, tn=1024):
    rows, d = c.shape
    n = w_ada.shape[1]
    return pl.pallas_call(
        _mod_body,
        grid=(n // tn,),
        in_specs=[pl.BlockSpec((rows, d), lambda j: (0, 0)),
                  pl.BlockSpec((d, tn), lambda j: (0, j)),
                  pl.BlockSpec((1, tn), lambda j: (0, j))],
        out_specs=pl.BlockSpec((rows, tn), lambda j: (0, j)),
        out_shape=jax.ShapeDtypeStruct((rows, n), F32),
        compiler_params=_params(1),
        name="modulation",
    )(c, w_ada, b_ada.reshape(1, n))


def _mod_spec(bb, d, which, n_axes):
    if n_axes == 3:
        return pl.BlockSpec((bb, None, 1, d), lambda b, i, j: (b, which, 0, 0))
    return pl.BlockSpec((bb, None, 1, d), lambda b, i: (b, which, 0, 0))


def _rope_tables(pos):
    half = ROT_DIM // 2
    inv_freq = ROPE_THETA ** (-(jnp.arange(half, dtype=F32) * 2.0) / ROT_DIM)
    ang = pos.astype(F32)[:, None] * inv_freq[None, :]
    cos, sin = jnp.cos(ang), jnp.sin(ang)
    t = pos.shape[0]
    ones = jnp.ones((t, QK_DIM - ROT_DIM), F32)
    zeros_h = jnp.zeros((t, half), F32)
    zeros_r = jnp.zeros((t, QK_DIM - ROT_DIM), F32)
    c = jnp.concatenate([cos, cos, ones], axis=1)
    s1 = jnp.concatenate([zeros_h, sin, zeros_r], axis=1)
    s2 = jnp.concatenate([-sin, zeros_h, zeros_r], axis=1)
    rep = LANES // QK_DIM
    return tuple(jnp.tile(a, (1, rep)) for a in (c, s1, s2))


def _qk_norm_rope(acc, g, c, s1, s2):
    m, tn = acc.shape
    lane = lax.broadcasted_iota(jnp.int32, (m, LANES), 1)
    lo = lane < QK_DIM
    outs = []
    for t in range(tn // LANES):
        z = acc[:, t * LANES:(t + 1) * LANES]
        zz = z * z
        s_lo = jnp.sum(jnp.where(lo, zz, 0.0), axis=-1, keepdims=True)
        s_hi = jnp.sum(jnp.where(lo, 0.0, zz), axis=-1, keepdims=True)
        r = jnp.where(lo, lax.rsqrt(s_lo * (1.0 / QK_DIM) + EPS),
                      lax.rsqrt(s_hi * (1.0 / QK_DIM) + EPS))
        y = z * r * g
        y = (y * c + pltpu.roll(y, ROT_DIM // 2, 1) * s1
             + pltpu.roll(y, LANES - ROT_DIM // 2, 1) * s2)
        outs.append(y)
    return jnp.concatenate(outs, axis=1)


def _gelu_tanh(x):
    return 0.5 * x * (1.0 + jnp.tanh(math.sqrt(2.0 / math.pi) * (x + 0.044715 * (x * x * x))))


def _lru_tile(ext, gb, h_prev, cw, cb, wax_ref, ba, bx, lam):
    tt = ext.shape[0] - SUBLANES
    w = ext.shape[1]
    n_blocks = wax_ref.shape[0]
    blk = w // n_blocks
    groups = tt // SUBLANES
    first = SUBLANES - (CONV_W - 1)
    u = cb + cw[0:1, :] * ext[pl.ds(first, tt), :]
    for jj in range(1, CONV_W):
        u = u + cw[jj:jj + 1, :] * ext[pl.ds(first + jj, tt), :]

    ub = u.astype(BF16)
    ga, gx = [], []
    for n in range(n_blocks):
        res = jnp.dot(ub[:, n * blk:(n + 1) * blk], wax_ref[n], preferred_element_type=F32)
        ga.append(res[:, :blk])
        gx.append(res[:, blk:])
    r = jax.nn.sigmoid(jnp.concatenate(ga, axis=1) + ba)
    ig = jax.nn.sigmoid(jnp.concatenate(gx, axis=1) + bx)

    nl = -lam
    softplus = jnp.maximum(nl, 0.0) + jnp.log1p(jnp.exp(-jnp.abs(nl)))
    log_a = -LRU_C * r * softplus
    a = jnp.exp(log_a)
    b = jnp.sqrt(-jnp.tanh(log_a) * (a * a + 1.0)) * (ig * u)

    a3 = a.reshape(groups, SUBLANES, w)
    b3 = b.reshape(groups, SUBLANES, w)
    row = lax.broadcasted_iota(jnp.int32, (groups, SUBLANES, w), 1)
    for s in (1, 2, 4):
        keep = row >= s
        b3 = jnp.where(keep, a3 * pltpu.roll(b3, s, 1) + b3, b3)
        a3 = jnp.where(keep, a3 * pltpu.roll(a3, s, 1), a3)
    hs = []
    for g in range(groups):
        hg = a3[g] * h_prev + b3[g]
        hs.append(hg)
        h_prev = hg[SUBLANES - 1:SUBLANES, :]
    h = jnp.concatenate(hs, axis=0)
    return h * _gelu_tanh(gb), h_prev


def _inproj_body(x_ref, sh_ref, sc_ref, w_ref, gq_ref, gk_ref, rc_ref, rs1_ref, rs2_ref,
                 h0_ref, c0_ref, cw_ref, cb_ref, wax_ref, ba_ref, bx_ref, lam_ref,
                 q_ref, k_ref, v_ref, y_ref, hn_ref, cn_ref,
                 xn_scr, ext_scr, gb_scr, h_scr, *, bb, tt, tn):
    i = pl.program_id(1)
    m = bb * tt
    d = x_ref.shape[-1]
    x = x_ref[...]
    ms = jnp.mean(x * x, axis=-1, keepdims=True)
    xn = x * lax.rsqrt(ms + EPS) * (1.0 + sc_ref[...]) + sh_ref[...]
    xn_scr[...] = xn.reshape(m, d).astype(BF16)

    @pl.when(i == 0)
    def _():
        ext_scr[:, 0:SUBLANES, :] = c0_ref[...]
        h_scr[...] = h0_ref[...]

    sec_w = q_ref.shape[-1]
    tiles = sec_w // tn

    def project(s, h):
        col = s * sec_w + h * tn
        return jnp.dot(xn_scr[...], w_ref[:, col:col + tn], preferred_element_type=F32)

    for h in range(tiles):
        ext_scr[:, SUBLANES:, h * tn:(h + 1) * tn] = project(3, h).reshape(bb, tt, tn)
    for h in range(tiles):
        gb_scr[:, h * tn:(h + 1) * tn] = project(4, h)
    for b in range(bb):
        y, h_last = _lru_tile(ext_scr.at[b], gb_scr[b * tt:(b + 1) * tt, :], h_scr[b],
                              cw_ref[...], cb_ref[...], wax_ref, ba_ref[...], bx_ref[...],
                              lam_ref[...])
        y_ref[b] = y.astype(BF16)
        h_scr[b] = h_last
    ext_scr[:, 0:SUBLANES, :] = ext_scr[:, tt:tt + SUBLANES, :]
    hn_ref[...] = h_scr[...]
    cn_ref[...] = ext_scr[:, 0:SUBLANES, :]

    tabs = (rc_ref[...], rs1_ref[...], rs2_ref[...])
    if bb > 1:
        tabs = tuple(jnp.concatenate([a] * bb, axis=0) for a in tabs)
    for h in range(tiles):
        acc = _qk_norm_rope(project(0, h), gq_ref[...], *tabs) * (QK_DIM ** -0.5)
        q_ref[:, :, h * tn:(h + 1) * tn] = acc.astype(BF16).reshape(bb, tt, tn)
    for h in range(tiles):
        acc = _qk_norm_rope(project(1, h), gk_ref[...], *tabs)
        k_ref[:, :, h * tn:(h + 1) * tn] = acc.reshape(bb, tt, tn)
    for h in range(tiles):
        v_ref[:, :, h * tn:(h + 1) * tn] = project(2, h).reshape(bb, tt, tn)


def _in_projection(x, mod, w_in, g_q, g_k, ropes, h0, conv0, lru_weights, bb, tt, tn=512):
    b, t, d = x.shape
    n = w_in.shape[1]
    sec_w = n // 5
    rc, rs1, rs2 = ropes
    conv_w, conv_b, w_ax, b_a, b_x, lru_lambda = lru_weights
    w = conv_w.shape[-1]
    blk = w // N_LRU_BLOCKS
    c0 = jnp.pad(conv0, ((0, 0), (SUBLANES - (CONV_W - 1), 0), (0, 0)))
    sec_spec = pl.BlockSpec((bb, tt, sec_w), lambda bi, i: (bi, i, 0))
    tab_spec = pl.BlockSpec((tt, LANES), lambda bi, i: (i, 0))
    g_spec = pl.BlockSpec((1, LANES), lambda bi, i: (0, 0))
    state_spec = lambda rows: pl.BlockSpec((bb, rows, w), lambda bi, i: (bi, 0, 0))
    vec_spec = lambda rows: pl.BlockSpec((rows, w), lambda bi, i: (0, 0))
    f32_out = jax.ShapeDtypeStruct((b, t, sec_w), F32)
    q, k, v, y, hn, cn = pl.pallas_call(
        functools.partial(_inproj_body, bb=bb, tt=tt, tn=tn),
        grid=(b // bb, t // tt),
        in_specs=[pl.BlockSpec((bb, tt, d), lambda bi, i: (bi, i, 0)),
                  _mod_spec(bb, d, 0, 2), _mod_spec(bb, d, 1, 2),
                  pl.BlockSpec((d, n), lambda bi, i: (0, 0), pipeline_mode=pl.Buffered(1)),
                  g_spec, g_spec, tab_spec, tab_spec, tab_spec,
                  state_spec(1), state_spec(SUBLANES), vec_spec(CONV_W), vec_spec(1),
                  pl.BlockSpec((N_LRU_BLOCKS, blk, 2 * blk), lambda bi, i: (0, 0, 0)),
                  vec_spec(1), vec_spec(1), vec_spec(1)],
        out_specs=[sec_spec, sec_spec, sec_spec, sec_spec, state_spec(1), state_spec(SUBLANES)],
        out_shape=[jax.ShapeDtypeStruct((b, t, sec_w), BF16), f32_out, f32_out,
                   jax.ShapeDtypeStruct((b, t, w), BF16),
                   jax.ShapeDtypeStruct((b, 1, w), F32),
                   jax.ShapeDtypeStruct((b, SUBLANES, w), F32)],
        scratch_shapes=[pltpu.VMEM((bb * tt, d), BF16),
                        pltpu.VMEM((bb, tt + SUBLANES, w), F32),
                        pltpu.VMEM((bb * tt, w), F32),
                        pltpu.VMEM((bb, 1, w), F32)],
        compiler_params=_params(2),
        name="in_projection",
    )(x, mod, mod, w_in, g_q, g_k, rc, rs1, rs2, h0.reshape(b, 1, w), c0,
      conv_w, conv_b.reshape(1, w), w_ax, b_a.reshape(1, w), b_x.reshape(1, w),
      lru_lambda.reshape(1, w))
    return q, k, v, y, hn[:, 0], cn[:, SUBLANES - (CONV_W - 1):]


def _lambda_value(lq1_ref, lk1_ref, lq2_ref, lk2_ref):
    s1 = jnp.sum(lq1_ref[...] * lk1_ref[...], axis=-1, keepdims=True)
    s2 = jnp.sum(lq2_ref[...] * lk2_ref[...], axis=-1, keepdims=True)
    return jnp.exp(s1) - jnp.exp(s2) + LAM_INIT


def _stack_components(q):
    lane = lax.broadcasted_iota(jnp.int32, q.shape, 1)
    zero = jnp.zeros_like(q)
    return jnp.concatenate([jnp.where(lane < QK_DIM, q, zero),
                            jnp.where(lane < QK_DIM, zero, q)], axis=0)


def _attn_prompt_body(lq1_ref, lk1_ref, lq2_ref, lk2_ref, g_ref, q_ref, k_ref, v_ref, o_ref,
                      kb_scr, vt_scr, s_scr, *, tq):
    t = q_ref.shape[1]
    nq = t // tq
    kb_scr[...] = (k_ref[0] * LOG2E).astype(BF16)
    vt_scr[...] = v_ref[0].T.astype(BF16)
    lam = _lambda_value(lq1_ref, lk1_ref, lq2_ref, lk2_ref)
    gain = g_ref[...] * (1.0 - LAM_INIT)

    key_chunk = lax.broadcasted_iota(jnp.int32, (tq, 2 * tq), 0) // CHUNK
    qcol = lax.broadcasted_iota(jnp.int32, (tq, 2 * tq), 1)
    q_chunk = jnp.where(qcol >= tq, qcol - tq, qcol) // CHUNK
    visible = key_chunk <= q_chunk

    def fold(x):
        return x.reshape(tq // SUBLANES, SUBLANES, x.shape[-1])

    def stacked_q(i):
        return _stack_components(q_ref[0, i * tq:(i + 1) * tq, :])

    def score_chunk(i, c, qs, m8):
        sc = lax.dot_general(kb_scr[c * tq:(c + 1) * tq, :], qs, (((1,), (1,)), ((), ())),
                             preferred_element_type=F32)
        if c == i:
            sc = jnp.where(visible, sc, -1e30)
        s_scr[i % 2, c * tq:(c + 1) * tq, :] = sc
        return jnp.maximum(m8, jnp.max(fold(sc), axis=0))

    def value_chunk(i, c, m, l8, ot):
        p = jnp.exp2(s_scr[i % 2, c * tq:(c + 1) * tq, :] - m)
        ot = ot + jnp.dot(vt_scr[:, c * tq:(c + 1) * tq], p.astype(BF16),
                          preferred_element_type=F32)
        return l8 + jnp.sum(fold(p), axis=0), ot

    neg_inf = jnp.full((SUBLANES, 2 * tq), -jnp.inf, F32)
    qs = stacked_q(0)
    m8 = score_chunk(0, 0, qs, neg_inf)
    for i in range(nq):
        m = jnp.max(m8, axis=0, keepdims=True)
        l8 = jnp.zeros((SUBLANES, 2 * tq), F32)
        ot = jnp.zeros((V_DIM, 2 * tq), F32)
        if i + 1 < nq:
            qs, m8 = stacked_q(i + 1), neg_inf
        for c in range(i + 2):
            if c <= i:
                l8, ot = value_chunk(i, c, m, l8, ot)
            if i + 1 < nq:
                m8 = score_chunk(i + 1, c, qs, m8)
        ot = ot / jnp.sum(l8, axis=0, keepdims=True)
        o = ot[:, :tq] - lam * ot[:, tq:]
        ms = jnp.mean(o * o, axis=0, keepdims=True)
        o = o * lax.rsqrt(ms + EPS) * gain
        o_ref[0, i * tq:(i + 1) * tq, :] = o.T.astype(BF16)


def _attention_prompt(q, k, v, lams, g_subln, tq=256):
    b, t, aw = q.shape
    head = pl.BlockSpec((1, t, V_DIM), lambda bi, h: (bi, 0, h))
    lam_spec = pl.BlockSpec((1, QK_DIM), lambda bi, h: (0, 0))
    return pl.pallas_call(
        functools.partial(_attn_prompt_body, tq=tq),
        grid=(b, N_HEADS),
        in_specs=[lam_spec] * 4 + [pl.BlockSpec((V_DIM, 1), lambda bi, h: (0, 0)),
                                   head, head, head],
        out_specs=head,
        out_shape=jax.ShapeDtypeStruct((b, t, aw), BF16),
        scratch_shapes=[pltpu.VMEM((t, V_DIM), BF16), pltpu.VMEM((V_DIM, t), BF16),
                        pltpu.VMEM((2, t, 2 * tq), F32)],
        compiler_params=_params(2),
        name="attention_prompt",
    )(*lams, g_subln.reshape(V_DIM, 1), q, k, v)


def _attn_sample_body(lq1_ref, lk1_ref, lq2_ref, lk2_ref, g_ref, q_ref, kt_ref, pv_ref,
                      nk_ref, nv_ref, o_ref, *, tq):
    lam = _lambda_value(lq1_ref, lk1_ref, lq2_ref, lk2_ref)
    gain = g_ref[...] * (1.0 - LAM_INIT)
    for h in range(N_HEADS):
        cols = slice(h * V_DIM, (h + 1) * V_DIM)
        qs = _stack_components(q_ref[0, :, cols])
        s_past = jnp.dot(qs, kt_ref[0, cols, :].astype(BF16), preferred_element_type=F32)
        s_new = lax.dot_general(qs, nk_ref[0, :, cols].astype(BF16), (((1,), (1,)), ((), ())),
                                preferred_element_type=F32)
        m = jnp.maximum(jnp.max(s_past, axis=-1, keepdims=True),
                        jnp.max(s_new, axis=-1, keepdims=True))
        p_past = jnp.exp(s_past - m)
        p_new = jnp.exp(s_new - m)
        l = jnp.sum(p_past, axis=-1, keepdims=True) + jnp.sum(p_new, axis=-1, keepdims=True)
        o = jnp.dot(p_past.astype(BF16), pv_ref[:, h, :].astype(BF16),
                    preferred_element_type=F32)
        o = o + jnp.dot(p_new.astype(BF16), nv_ref[0, :, cols].astype(BF16),
                        preferred_element_type=F32)
        o = o / l
        o = o[:tq] - lam * o[tq:]
        ms = jnp.mean(o * o, axis=-1, keepdims=True)
        o_ref[0, :, cols] = (o * lax.rsqrt(ms + EPS) * gain).astype(BF16)


def _attention_sample(q, past_kt, past_v, new_k, new_v, lams, g_subln):
    b, t, aw = q.shape
    past = past_kt.shape[-1]
    assert past % CHUNK == 0 and t <= CHUNK
    row_new = pl.BlockSpec((1, t, aw), lambda bi: (bi, 0, 0))
    vec = lambda width: pl.BlockSpec((1, width), lambda bi: (0, 0))
    return pl.pallas_call(
        functools.partial(_attn_sample_body, tq=t),
        grid=(b,),
        in_specs=[vec(QK_DIM)] * 4 + [
            vec(V_DIM), row_new,
            pl.BlockSpec((1, aw, past), lambda bi: (bi, 0, 0)),
            pl.BlockSpec((None, past, N_HEADS, V_DIM), lambda bi: (bi, 0, 0, 0)),
            row_new, row_new],
        out_specs=row_new,
        out_shape=jax.ShapeDtypeStruct((b, t, aw), BF16),
        compiler_params=_params(1),
        name="attention_sample",
    )(*lams, g_subln, q, past_kt, past_v, new_k, new_v)


def _mlp_body(x_ref, o_ref, y_ref, gt1_ref, sh2_ref, sc2_ref, gt2_ref, wo_ref, wu_ref, wd_ref,
              out_ref, xn_scr, acc_scr, *, bb, tt):
    f = pl.program_id(2)
    m = bb * tt
    d = x_ref.shape[-1]
    aw = o_ref.shape[-1]

    @pl.when(f == 0)
    def _():
        mix = jnp.dot(o_ref[...].reshape(m, aw), wo_ref[0:aw, :], preferred_element_type=F32)
        mix = mix + jnp.dot(y_ref[...].reshape(m, y_ref.shape[-1]), wo_ref[aw:, :],
                            preferred_element_type=F32)
        x1 = x_ref[...] + gt1_ref[...] * mix.reshape(bb, tt, d)
        out_ref[...] = x1
        ms = jnp.mean(x1 * x1, axis=-1, keepdims=True)
        xn = x1 * lax.rsqrt(ms + EPS) * (1.0 + sc2_ref[...]) + sh2_ref[...]
        xn_scr[...] = xn.reshape(m, d).astype(BF16)
        acc_scr[...] = jnp.zeros(acc_scr.shape, F32)

    hid = jnp.maximum(jnp.dot(xn_scr[...], wu_ref[...], preferred_element_type=F32), 0.0)
    acc_scr[...] += jnp.dot((hid * hid).astype(BF16), wd_ref[...], preferred_element_type=F32)

    @pl.when(f == pl.num_programs(2) - 1)
    def _():
        out_ref[...] = out_ref[...] + gt2_ref[...] * acc_scr[...].reshape(bb, tt, d)


def _outproj_mlp(x, o, y, mod, w_out, w_up, w_down, bb, tt, tf=1024):
    b, t, d = x.shape
    aw, lw = o.shape[-1], y.shape[-1]
    dff = w_up.shape[1]
    row = lambda width: pl.BlockSpec((bb, tt, width), lambda bi, i, f: (bi, i, 0))
    return pl.pallas_call(
        functools.partial(_mlp_body, bb=bb, tt=tt),
        grid=(b // bb, t // tt, dff // tf),
        in_specs=[row(d), row(aw), row(lw),
                  _mod_spec(bb, d, 2, 3), _mod_spec(bb, d, 3, 3),
                  _mod_spec(bb, d, 4, 3), _mod_spec(bb, d, 5, 3),
                  pl.BlockSpec((aw + lw, d), lambda bi, i, f: (0, 0),
                               pipeline_mode=pl.Buffered(1)),
                  pl.BlockSpec((d, tf), lambda bi, i, f: (0, f)),
                  pl.BlockSpec((tf, d), lambda bi, i, f: (f, 0))],
        out_specs=row(d),
        out_shape=jax.ShapeDtypeStruct((b, t, d), F32),
        scratch_shapes=[pltpu.VMEM((bb * tt, d), BF16), pltpu.VMEM((bb * tt, d), F32)],
        compiler_params=_params(3),
        name="outproj_mlp",
    )(x, o, y, mod, mod, mod, mod, w_out, w_up, w_down)


def _layer_common(x, mod, pos, h0, conv0, weights, bb, tt):
    (w_in, g_q, g_k, *lru_weights) = weights
    return _in_projection(x, mod, w_in, g_q, g_k, _rope_tables(pos), h0, conv0, lru_weights,
                          bb, tt)


def kernel(x_prompt, x_sample, c_prompt, c_sample, cache_k, cache_v, state_lru_h, state_conv,
           w_ada, b_ada, w_in, g_q, g_k, lambda_q1, lambda_k1, lambda_q2, lambda_k2, g_subln,
           conv_w, conv_b, w_gate_a, b_gate_a, w_gate_x, b_gate_x, lru_lambda,
           w_out, w_up, w_down):
    bp, tp, d = x_prompt.shape
    bs, ts, _ = x_sample.shape
    past = cache_k.shape[2]
    aw = N_HEADS * V_DIM
    lw = conv_w.shape[-1]

    c_all = jnp.concatenate([c_prompt, c_sample], axis=0)
    mod = _modulation(c_all, w_ada[0], b_ada[0]).reshape(bp + bs, N_MOD, 1, d)
    mod_p, mod_s = mod[:bp], mod[bp:]

    rep = LANES // QK_DIM
    weights = (w_in[0].astype(BF16),
               jnp.tile(g_q[0], rep).reshape(1, LANES), jnp.tile(g_k[0], rep).reshape(1, LANES),
               conv_w[0], conv_b[0],
               jnp.concatenate([w_gate_a[0], w_gate_x[0]], axis=-1).astype(BF16),
               b_gate_a[0].reshape(-1), b_gate_x[0].reshape(-1), lru_lambda[0])
    lams = tuple(a[0].reshape(1, QK_DIM) for a in (lambda_q1, lambda_k1, lambda_q2, lambda_k2))
    g_sub = g_subln[0].reshape(1, V_DIM)
    wo, wu, wd = w_out[0].astype(BF16), w_up[0].astype(BF16), w_down[0].astype(BF16)

    tt_p = min(tp, 512)
    qp, kp, vp, yp, hp, cp = _layer_common(
        x_prompt, mod_p, jnp.arange(tp), jnp.zeros((bp, lw), F32),
        jnp.zeros((bp, CONV_W - 1, lw), F32), weights, 1, tt_p)
    op = _attention_prompt(qp, kp, vp, lams, g_sub, tq=min(tp, 512))
    out_p = _outproj_mlp(x_prompt, op, yp, mod_p, wo, wu, wd, 1, tt_p)

    qs, ks, vs, ys, hs, cs = _layer_common(
        x_sample, mod_s, past + jnp.arange(ts), state_lru_h[0], state_conv[0], weights,
        bs, ts)
    past_kt = jnp.transpose(cache_k[0], (0, 2, 3, 4, 1)).reshape(bs, aw, past)
    osamp = _attention_sample(qs, past_kt, cache_v[0], ks, vs, lams, g_sub)
    out_s = _outproj_mlp(x_sample, osamp, ys, mod_s, wo, wu, wd, bs, ts)

    return (out_p, out_s,
            kp.reshape(1, bp, tp, N_HEADS, 2, QK_DIM), vp.reshape(1, bp, tp, N_HEADS, V_DIM),
            hp[None], cp[None],
            ks.reshape(1, bs, ts, N_HEADS, 2, QK_DIM), vs.reshape(1, bs, ts, N_HEADS, V_DIM),
            hs[None], cs[None])
```

```python
import functools
import math

import jax
import jax.numpy as jnp
from jax import lax
from jax.experimental import pallas as pl
from jax.experimental.pallas import tpu as pltpu

F32 = jnp.float32
BF16 = jnp.bfloat16

LANES = 128
SUBLANES = 8
VMEM_LIMIT = 56 * 1024 * 1024

ROW_TILE = 512
PROJ_COL_TILE = 512
MLP_HIDDEN_TILE = 1024
Q_BLOCK = 512
MOD_COL_TILE = 1024

N_HEADS = 8
QK_DIM = 64
V_DIM = 2 * QK_DIM
CHUNK = 64
ROT_DIM = QK_DIM // 4
ROPE_THETA = 500000.0
N_LRU_BLOCKS = 8
CONV_W = 4
LRU_C = 8.0
N_MOD = 6
EPS = 1e-6
LAM_INIT = 0.8 - 0.6 * math.exp(-0.3 * 0)
LOG2E = math.log2(math.e)


def _params(n_axes):
    return pltpu.CompilerParams(dimension_semantics=("arbitrary",) * n_axes,
                                vmem_limit_bytes=VMEM_LIMIT)


def _mod_body(c_ref, w_ref, b_ref, o_ref):
    c = c_ref[...]
    s = c * jax.nn.sigmoid(c)
    o_ref[...] = jnp.dot(s.astype(BF16), w_ref[...].astype(BF16),
                         preferred_element_type=F32) + b_ref[...]


def _modulation(c, w_ada, b_ada, tn=MOD_COL_TILE):
    rows, d = c.shape
    n = w_ada.shape[1]
    return pl.pallas_call(
        _mod_body,
        grid=(n // tn,),
        in_specs=[pl.BlockSpec((rows, d), lambda j: (0, 0)),
                  pl.BlockSpec((d, tn), lambda j: (0, j)),
                  pl.BlockSpec((1, tn), lambda j: (0, j))],
        out_specs=pl.BlockSpec((rows, tn), lambda j: (0, j)),
        out_shape=jax.ShapeDtypeStruct((rows, n), F32),
        compiler_params=_params(1),
        name="modulation",
    )(c, w_ada, b_ada.reshape(1, n))


def _mod_spec(bb, d, which, n_axes):
    if n_axes == 3:
        return pl.BlockSpec((bb, None, 1, d), lambda b, i, j: (b, which, 0, 0))
    return pl.BlockSpec((bb, None, 1, d), lambda b, i: (b, which, 0, 0))


def _rope_tables(pos):
    half = ROT_DIM // 2
    inv_freq = ROPE_THETA ** (-(jnp.arange(half, dtype=F32) * 2.0) / ROT_DIM)
    ang = pos.astype(F32)[:, None] * inv_freq[None, :]
    cos, sin = jnp.cos(ang), jnp.sin(ang)
    t = pos.shape[0]
    ones = jnp.ones((t, QK_DIM - ROT_DIM), F32)
    zeros_h = jnp.zeros((t, half), F32)
    zeros_r = jnp.zeros((t, QK_DIM - ROT_DIM), F32)
    c = jnp.concatenate([cos, cos, ones], axis=1)
    s1 = jnp.concatenate([zeros_h, sin, zeros_r], axis=1)
    s2 = jnp.concatenate([-sin, zeros_h, zeros_r], axis=1)
    rep = LANES // QK_DIM
    return tuple(jnp.tile(a, (1, rep)) for a in (c, s1, s2))


def _qk_norm_rope(acc, g, c, s1, s2):
    m, tn = acc.shape
    lane = lax.broadcasted_iota(jnp.int32, (m, LANES), 1)
    lo = lane < QK_DIM
    outs = []
    for t in range(tn // LANES):
        z = acc[:, t * LANES:(t + 1) * LANES]
        zz = z * z
        s_lo = jnp.sum(jnp.where(lo, zz, 0.0), axis=-1, keepdims=True)
        s_hi = jnp.sum(jnp.where(lo, 0.0, zz), axis=-1, keepdims=True)
        r = jnp.where(lo, lax.rsqrt(s_lo * (1.0 / QK_DIM) + EPS),
                      lax.rsqrt(s_hi * (1.0 / QK_DIM) + EPS))
        y = z * r * g
        y = (y * c + pltpu.roll(y, ROT_DIM // 2, 1) * s1
             + pltpu.roll(y, LANES - ROT_DIM // 2, 1) * s2)
        outs.append(y)
    return jnp.concatenate(outs, axis=1)


def _gelu_tanh(x):
    return 0.5 * x * (1.0 + jnp.tanh(math.sqrt(2.0 / math.pi) * (x + 0.044715 * (x * x * x))))


def _lru_tile(ext, gb, h_prev, cw, cb, wax_ref, ba, bx, lam):
    tt = ext.shape[0] - SUBLANES
    w = ext.shape[1]
    n_blocks = wax_ref.shape[0]
    blk = w // n_blocks
    groups = tt // SUBLANES
    first = SUBLANES - (CONV_W - 1)
    u = cb + cw[0:1, :] * ext[pl.ds(first, tt), :]
    for jj in range(1, CONV_W):
        u = u + cw[jj:jj + 1, :] * ext[pl.ds(first + jj, tt), :]

    ub = u.astype(BF16)
    ga, gx = [], []
    for n in range(n_blocks):
        res = jnp.dot(ub[:, n * blk:(n + 1) * blk], wax_ref[n], preferred_element_type=F32)
        ga.append(res[:, :blk])
        gx.append(res[:, blk:])
    r = jax.nn.sigmoid(jnp.concatenate(ga, axis=1) + ba)
    ig = jax.nn.sigmoid(jnp.concatenate(gx, axis=1) + bx)

    nl = -lam
    softplus = jnp.maximum(nl, 0.0) + jnp.log1p(jnp.exp(-jnp.abs(nl)))
    log_a = -LRU_C * r * softplus
    a = jnp.exp(log_a)
    b = jnp.sqrt(-jnp.tanh(log_a) * (a * a + 1.0)) * (ig * u)

    a3 = a.reshape(groups, SUBLANES, w)
    b3 = b.reshape(groups, SUBLANES, w)
    row = lax.broadcasted_iota(jnp.int32, (groups, SUBLANES, w), 1)
    for s in (1, 2, 4):
        keep = row >= s
        b3 = jnp.where(keep, a3 * pltpu.roll(b3, s, 1) + b3, b3)
        a3 = jnp.where(keep, a3 * pltpu.roll(a3, s, 1), a3)
    hs = []
    for g in range(groups):
        hg = a3[g] * h_prev + b3[g]
        hs.append(hg)
        h_prev = hg[SUBLANES - 1:SUBLANES, :]
    h = jnp.concatenate(hs, axis=0)
    return h * _gelu_tanh(gb), h_prev


def _inproj_body(x_ref, sh_ref, sc_ref, w_ref, gq_ref, gk_ref, rc_ref, rs1_ref, rs2_ref,
                 h0_ref, c0_ref, cw_ref, cb_ref, wax_ref, ba_ref, bx_ref, lam_ref,
                 q_ref, k_ref, v_ref, y_ref, hn_ref, cn_ref,
                 xn_scr, ext_scr, gb_scr, h_scr, *, bb, tt, tn):
    i = pl.program_id(1)
    m = bb * tt
    d = x_ref.shape[-1]
    x = x_ref[...]
    ms = jnp.mean(x * x, axis=-1, keepdims=True)
    xn = x * lax.rsqrt(ms + EPS) * (1.0 + sc_ref[...]) + sh_ref[...]
    xn_scr[...] = xn.reshape(m, d).astype(BF16)

    @pl.when(i == 0)
    def _():
        ext_scr[:, 0:SUBLANES, :] = c0_ref[...]
        h_scr[...] = h0_ref[...]

    sec_w = q_ref.shape[-1]
    tiles = sec_w // tn

    def project(s, h):
        col = s * sec_w + h * tn
        return jnp.dot(xn_scr[...], w_ref[:, col:col + tn], preferred_element_type=F32)

    for h in range(tiles):
        ext_scr[:, SUBLANES:, h * tn:(h + 1) * tn] = project(3, h).reshape(bb, tt, tn)
    for h in range(tiles):
        gb_scr[:, h * tn:(h + 1) * tn] = project(4, h)
    for b in range(bb):
        y, h_last = _lru_tile(ext_scr.at[b], gb_scr[b * tt:(b + 1) * tt, :], h_scr[b],
                              cw_ref[...], cb_ref[...], wax_ref, ba_ref[...], bx_ref[...],
                              lam_ref[...])
        y_ref[b] = y.astype(BF16)
        h_scr[b] = h_last
    ext_scr[:, 0:SUBLANES, :] = ext_scr[:, tt:tt + SUBLANES, :]
    hn_ref[...] = h_scr[...]
    cn_ref[...] = ext_scr[:, 0:SUBLANES, :]

    tabs = (rc_ref[...], rs1_ref[...], rs2_ref[...])
    if bb > 1:
        tabs = tuple(jnp.concatenate([a] * bb, axis=0) for a in tabs)
    for h in range(tiles):
        acc = _qk_norm_rope(project(0, h), gq_ref[...], *tabs) * (QK_DIM ** -0.5)
        q_ref[:, :, h * tn:(h + 1) * tn] = acc.astype(BF16).reshape(bb, tt, tn)
    for h in range(tiles):
        acc = _qk_norm_rope(project(1, h), gk_ref[...], *tabs)
        k_ref[:, :, h * tn:(h + 1) * tn] = acc.reshape(bb, tt, tn)
    for h in range(tiles):
        v_ref[:, :, h * tn:(h + 1) * tn] = project(2, h).reshape(bb, tt, tn)


def _in_projection(x, mod, w_in, g_q, g_k, ropes, h0, conv0, lru_weights, bb, tt,
                   tn=PROJ_COL_TILE):
    b, t, d = x.shape
    n = w_in.shape[1]
    sec_w = n // 5
    rc, rs1, rs2 = ropes
    conv_w, conv_b, w_ax, b_a, b_x, lru_lambda = lru_weights
    w = conv_w.shape[-1]
    blk = w // N_LRU_BLOCKS
    c0 = jnp.pad(conv0, ((0, 0), (SUBLANES - (CONV_W - 1), 0), (0, 0)))
    sec_spec = pl.BlockSpec((bb, tt, sec_w), lambda bi, i: (bi, i, 0))
    tab_spec = pl.BlockSpec((tt, LANES), lambda bi, i: (i, 0))
    g_spec = pl.BlockSpec((1, LANES), lambda bi, i: (0, 0))
    state_spec = lambda rows: pl.BlockSpec((bb, rows, w), lambda bi, i: (bi, 0, 0))
    vec_spec = lambda rows: pl.BlockSpec((rows, w), lambda bi, i: (0, 0))
    f32_out = jax.ShapeDtypeStruct((b, t, sec_w), F32)
    q, k, v, y, hn, cn = pl.pallas_call(
        functools.partial(_inproj_body, bb=bb, tt=tt, tn=tn),
        grid=(b // bb, t // tt),
        in_specs=[pl.BlockSpec((bb, tt, d), lambda bi, i: (bi, i, 0)),
                  _mod_spec(bb, d, 0, 2), _mod_spec(bb, d, 1, 2),
                  pl.BlockSpec((d, n), lambda bi, i: (0, 0), pipeline_mode=pl.Buffered(1)),
                  g_spec, g_spec, tab_spec, tab_spec, tab_spec,
                  state_spec(1), state_spec(SUBLANES), vec_spec(CONV_W), vec_spec(1),
                  pl.BlockSpec((N_LRU_BLOCKS, blk, 2 * blk), lambda bi, i: (0, 0, 0)),
                  vec_spec(1), vec_spec(1), vec_spec(1)],
        out_specs=[sec_spec, sec_spec, sec_spec, sec_spec, state_spec(1), state_spec(SUBLANES)],
        out_shape=[jax.ShapeDtypeStruct((b, t, sec_w), BF16), f32_out, f32_out,
                   jax.ShapeDtypeStruct((b, t, w), BF16),
                   jax.ShapeDtypeStruct((b, 1, w), F32),
                   jax.ShapeDtypeStruct((b, SUBLANES, w), F32)],
        scratch_shapes=[pltpu.VMEM((bb * tt, d), BF16),
                        pltpu.VMEM((bb, tt + SUBLANES, w), F32),
                        pltpu.VMEM((bb * tt, w), F32),
                        pltpu.VMEM((bb, 1, w), F32)],
        compiler_params=_params(2),
        name="in_projection",
    )(x, mod, mod, w_in, g_q, g_k, rc, rs1, rs2, h0.reshape(b, 1, w), c0,
      conv_w, conv_b.reshape(1, w), w_ax, b_a.reshape(1, w), b_x.reshape(1, w),
      lru_lambda.reshape(1, w))
    return q, k, v, y, hn[:, 0], cn[:, SUBLANES - (CONV_W - 1):]


def _lambda_value(lq1_ref, lk1_ref, lq2_ref, lk2_ref):
    s1 = jnp.sum(lq1_ref[...] * lk1_ref[...], axis=-1, keepdims=True)
    s2 = jnp.sum(lq2_ref[...] * lk2_ref[...], axis=-1, keepdims=True)
    return jnp.exp(s1) - jnp.exp(s2) + LAM_INIT


def _stack_components(q):
    lane = lax.broadcasted_iota(jnp.int32, q.shape, 1)
    zero = jnp.zeros_like(q)
    return jnp.concatenate([jnp.where(lane < QK_DIM, q, zero),
                            jnp.where(lane < QK_DIM, zero, q)], axis=0)


def _attn_prompt_body(lq1_ref, lk1_ref, lq2_ref, lk2_ref, g_ref, q_ref, k_ref, v_ref,
                      wo_ref, wu_ref, wd_ref, o_ref, wo_bf_ref, wu_bf_ref, wd_bf_ref,
                      kb_scr, vt_scr, s_scr, *, tq):
    for src, dst in ((wo_ref, wo_bf_ref), (wu_ref, wu_bf_ref), (wd_ref, wd_bf_ref)):
        dst[...] = src[...].astype(BF16)
    t = q_ref.shape[1]
    nq = t // tq
    kb_scr[...] = (k_ref[0] * LOG2E).astype(BF16)
    vt_scr[...] = v_ref[0].T.astype(BF16)
    lam = _lambda_value(lq1_ref, lk1_ref, lq2_ref, lk2_ref)
    gain = g_ref[...] * (1.0 - LAM_INIT)

    key_chunk = lax.broadcasted_iota(jnp.int32, (tq, 2 * tq), 0) // CHUNK
    qcol = lax.broadcasted_iota(jnp.int32, (tq, 2 * tq), 1)
    q_chunk = jnp.where(qcol >= tq, qcol - tq, qcol) // CHUNK
    visible = key_chunk <= q_chunk

    def fold(x):
        return x.reshape(tq // SUBLANES, SUBLANES, x.shape[-1])

    def stacked_q(i):
        return _stack_components(q_ref[0, i * tq:(i + 1) * tq, :])

    def score_chunk(i, c, qs, m8):
        sc = lax.dot_general(kb_scr[c * tq:(c + 1) * tq, :], qs, (((1,), (1,)), ((), ())),
                             preferred_element_type=F32)
        if c == i:
            sc = jnp.where(visible, sc, -1e30)
        s_scr[i % 2, c * tq:(c + 1) * tq, :] = sc
        return jnp.maximum(m8, jnp.max(fold(sc), axis=0))

    def value_chunk(i, c, m, l8, ot):
        p = jnp.exp2(s_scr[i % 2, c * tq:(c + 1) * tq, :] - m)
        ot = ot + jnp.dot(vt_scr[:, c * tq:(c + 1) * tq], p.astype(BF16),
                          preferred_element_type=F32)
        return l8 + jnp.sum(fold(p), axis=0), ot

    neg_inf = jnp.full((SUBLANES, 2 * tq), -jnp.inf, F32)
    qs = stacked_q(0)
    m8 = score_chunk(0, 0, qs, neg_inf)
    for i in range(nq):
        m = jnp.max(m8, axis=0, keepdims=True)
        l8 = jnp.zeros((SUBLANES, 2 * tq), F32)
        ot = jnp.zeros((V_DIM, 2 * tq), F32)
        if i + 1 < nq:
            qs, m8 = stacked_q(i + 1), neg_inf
        for c in range(i + 2):
            if c <= i:
                l8, ot = value_chunk(i, c, m, l8, ot)
            if i + 1 < nq:
                m8 = score_chunk(i + 1, c, qs, m8)
        ot = ot / jnp.sum(l8, axis=0, keepdims=True)
        o = ot[:, :tq] - lam * ot[:, tq:]
        ms = jnp.mean(o * o, axis=0, keepdims=True)
        o = o * lax.rsqrt(ms + EPS) * gain
        o_ref[0, i * tq:(i + 1) * tq, :] = o.T.astype(BF16)


def _attention_prompt(q, k, v, lams, g_subln, f32_weights, tq):
    b, t, aw = q.shape
    n_steps = b * N_HEADS
    head = pl.BlockSpec((1, t, V_DIM), lambda bi, h: (bi, 0, h))
    lam_spec = pl.BlockSpec((1, QK_DIM), lambda bi, h: (0, 0))

    def slice_spec(wt):
        rows = wt.shape[0] // n_steps
        assert rows * n_steps == wt.shape[0] and rows % (2 * SUBLANES) == 0
        return pl.BlockSpec((rows, wt.shape[1]), lambda bi, h: (bi * N_HEADS + h, 0))

    cast_specs = [slice_spec(wt) for wt in f32_weights]
    out, *casts = pl.pallas_call(
        functools.partial(_attn_prompt_body, tq=tq),
        grid=(b, N_HEADS),
        in_specs=[lam_spec] * 4 + [pl.BlockSpec((V_DIM, 1), lambda bi, h: (0, 0)),
                                   head, head, head] + cast_specs,
        out_specs=[head] + cast_specs,
        out_shape=[jax.ShapeDtypeStruct((b, t, aw), BF16)]
        + [jax.ShapeDtypeStruct(wt.shape, BF16) for wt in f32_weights],
        scratch_shapes=[pltpu.VMEM((t, V_DIM), BF16), pltpu.VMEM((V_DIM, t), BF16),
                        pltpu.VMEM((2, t, 2 * tq), F32)],
        compiler_params=_params(2),
        name="attention_prompt",
    )(*lams, g_subln.reshape(V_DIM, 1), q, k, v, *f32_weights)
    return out, casts


def _attn_sample_body(lq1_ref, lk1_ref, lq2_ref, lk2_ref, g_ref, q_ref, kt_ref, pv_ref,
                      nk_ref, nv_ref, o_ref, *, tq):
    lam = _lambda_value(lq1_ref, lk1_ref, lq2_ref, lk2_ref)
    gain = g_ref[...] * (1.0 - LAM_INIT)
    for h in range(N_HEADS):
        cols = slice(h * V_DIM, (h + 1) * V_DIM)
        qs = _stack_components(q_ref[0, :, cols])
        s_past = jnp.dot(qs, kt_ref[0, cols, :].astype(BF16), preferred_element_type=F32)
        s_new = lax.dot_general(qs, nk_ref[0, :, cols].astype(BF16), (((1,), (1,)), ((), ())),
                                preferred_element_type=F32)
        m = jnp.maximum(jnp.max(s_past, axis=-1, keepdims=True),
                        jnp.max(s_new, axis=-1, keepdims=True))
        p_past = jnp.exp(s_past - m)
        p_new = jnp.exp(s_new - m)
        l = jnp.sum(p_past, axis=-1, keepdims=True) + jnp.sum(p_new, axis=-1, keepdims=True)
        o = jnp.dot(p_past.astype(BF16), pv_ref[:, h, :].astype(BF16),
                    preferred_element_type=F32)
        o = o + jnp.dot(p_new.astype(BF16), nv_ref[0, :, cols].astype(BF16),
                        preferred_element_type=F32)
        o = o / l
        o = o[:tq] - lam * o[tq:]
        ms = jnp.mean(o * o, axis=-1, keepdims=True)
        o_ref[0, :, cols] = (o * lax.rsqrt(ms + EPS) * gain).astype(BF16)


def _attention_sample(q, past_kt, past_v, new_k, new_v, lams, g_subln):
    b, t, aw = q.shape
    past = past_kt.shape[-1]
    assert past % CHUNK == 0 and t <= CHUNK
    row_new = pl.BlockSpec((1, t, aw), lambda bi: (bi, 0, 0))
    vec = lambda width: pl.BlockSpec((1, width), lambda bi: (0, 0))
    return pl.pallas_call(
        functools.partial(_attn_sample_body, tq=t),
        grid=(b,),
        in_specs=[vec(QK_DIM)] * 4 + [
            vec(V_DIM), row_new,
            pl.BlockSpec((1, aw, past), lambda bi: (bi, 0, 0)),
            pl.BlockSpec((None, past, N_HEADS, V_DIM), lambda bi: (bi, 0, 0, 0)),
            row_new, row_new],
        out_specs=row_new,
        out_shape=jax.ShapeDtypeStruct((b, t, aw), BF16),
        compiler_params=_params(1),
        name="attention_sample",
    )(*lams, g_subln, q, past_kt, past_v, new_k, new_v)


def _mlp_body(x_ref, o_ref, y_ref, gt1_ref, sh2_ref, sc2_ref, gt2_ref, wo_ref, wu_ref, wd_ref,
              out_ref, xn_scr, acc_scr, *, bb, tt):
    f = pl.program_id(2)
    m = bb * tt
    d = x_ref.shape[-1]
    aw = o_ref.shape[-1]

    @pl.when(f == 0)
    def _():
        mix = jnp.dot(o_ref[...].reshape(m, aw), wo_ref[0:aw, :], preferred_element_type=F32)
        mix = mix + jnp.dot(y_ref[...].reshape(m, y_ref.shape[-1]), wo_ref[aw:, :],
                            preferred_element_type=F32)
        x1 = x_ref[...] + gt1_ref[...] * mix.reshape(bb, tt, d)
        out_ref[...] = x1
        ms = jnp.mean(x1 * x1, axis=-1, keepdims=True)
        xn = x1 * lax.rsqrt(ms + EPS) * (1.0 + sc2_ref[...]) + sh2_ref[...]
        xn_scr[...] = xn.reshape(m, d).astype(BF16)
        acc_scr[...] = jnp.zeros(acc_scr.shape, F32)

    hid = jnp.maximum(jnp.dot(xn_scr[...], wu_ref[...], preferred_element_type=F32), 0.0)
    acc_scr[...] += jnp.dot((hid * hid).astype(BF16), wd_ref[...], preferred_element_type=F32)

    @pl.when(f == pl.num_programs(2) - 1)
    def _():
        out_ref[...] = out_ref[...] + gt2_ref[...] * acc_scr[...].reshape(bb, tt, d)


def _outproj_mlp(x, o, y, mod, w_out, w_up, w_down, bb, tt, tf=MLP_HIDDEN_TILE):
    b, t, d = x.shape
    aw, lw = o.shape[-1], y.shape[-1]
    dff = w_up.shape[1]
    row = lambda width: pl.BlockSpec((bb, tt, width), lambda bi, i, f: (bi, i, 0))
    return pl.pallas_call(
        functools.partial(_mlp_body, bb=bb, tt=tt),
        grid=(b // bb, t // tt, dff // tf),
        in_specs=[row(d), row(aw), row(lw),
                  _mod_spec(bb, d, 2, 3), _mod_spec(bb, d, 3, 3),
                  _mod_spec(bb, d, 4, 3), _mod_spec(bb, d, 5, 3),
                  pl.BlockSpec((aw + lw, d), lambda bi, i, f: (0, 0),
                               pipeline_mode=pl.Buffered(1)),
                  pl.BlockSpec((d, tf), lambda bi, i, f: (0, f)),
                  pl.BlockSpec((tf, d), lambda bi, i, f: (f, 0))],
        out_specs=row(d),
        out_shape=jax.ShapeDtypeStruct((b, t, d), F32),
        scratch_shapes=[pltpu.VMEM((bb * tt, d), BF16), pltpu.VMEM((bb * tt, d), F32)],
        compiler_params=_params(3),
        name="outproj_mlp",
    )(x, o, y, mod, mod, mod, mod, w_out, w_up, w_down)


def _layer_common(x, mod, pos, h0, conv0, weights, bb, tt):
    (w_in, g_q, g_k, *lru_weights) = weights
    return _in_projection(x, mod, w_in, g_q, g_k, _rope_tables(pos), h0, conv0, lru_weights,
                          bb, tt)


def kernel(x_prompt, x_sample, c_prompt, c_sample, cache_k, cache_v, state_lru_h, state_conv,
           w_ada, b_ada, w_in, g_q, g_k, lambda_q1, lambda_k1, lambda_q2, lambda_k2, g_subln,
           conv_w, conv_b, w_gate_a, b_gate_a, w_gate_x, b_gate_x, lru_lambda,
           w_out, w_up, w_down):
    bp, tp, d = x_prompt.shape
    bs, ts, _ = x_sample.shape
    past = cache_k.shape[2]
    aw = N_HEADS * V_DIM
    lw = conv_w.shape[-1]

    c_all = jnp.concatenate([c_prompt, c_sample], axis=0)
    mod = _modulation(c_all, w_ada[0], b_ada[0]).reshape(bp + bs, N_MOD, 1, d)
    mod_p, mod_s = mod[:bp], mod[bp:]

    rep = LANES // QK_DIM
    weights = (w_in[0].astype(BF16),
               jnp.tile(g_q[0], rep).reshape(1, LANES), jnp.tile(g_k[0], rep).reshape(1, LANES),
               conv_w[0], conv_b[0],
               jnp.concatenate([w_gate_a[0], w_gate_x[0]], axis=-1).astype(BF16),
               b_gate_a[0].reshape(-1), b_gate_x[0].reshape(-1), lru_lambda[0])
    lams = tuple(a[0].reshape(1, QK_DIM) for a in (lambda_q1, lambda_k1, lambda_q2, lambda_k2))
    g_sub = g_subln[0].reshape(1, V_DIM)

    tt_p = min(tp, ROW_TILE)
    qp, kp, vp, yp, hp, cp = _layer_common(
        x_prompt, mod_p, jnp.arange(tp), jnp.zeros((bp, lw), F32),
        jnp.zeros((bp, CONV_W - 1, lw), F32), weights, 1, tt_p)
    op, (wo, wu, wd) = _attention_prompt(qp, kp, vp, lams, g_sub,
                                         (w_out[0], w_up[0], w_down[0]), tq=min(tp, Q_BLOCK))
    out_p = _outproj_mlp(x_prompt, op, yp, mod_p, wo, wu, wd, 1, tt_p)

    qs, ks, vs, ys, hs, cs = _layer_common(
        x_sample, mod_s, past + jnp.arange(ts), state_lru_h[0], state_conv[0], weights,
        bs, ts)
    past_kt = jnp.transpose(cache_k[0], (0, 2, 3, 4, 1)).reshape(bs, aw, past)
    osamp = _attention_sample(qs, past_kt, cache_v[0], ks, vs, lams, g_sub)
    out_s = _outproj_mlp(x_sample, osamp, ys, mod_s, wo, wu, wd, bs, ts)

    return (out_p, out_s,
            kp.reshape(1, bp, tp, N_HEADS, 2, QK_DIM), vp.reshape(1, bp, tp, N_HEADS, V_DIM),
            hp[None], cp[None],
            ks.reshape(1, bs, ts, N_HEADS, 2, QK_DIM), vs.reshape(1, bs, ts, N_HEADS, V_DIM),
            hs[None], cs[None])
```

```python
import functools
import math

import jax
import jax.numpy as jnp
from jax import lax
from jax.experimental import pallas as pl
from jax.experimental.pallas import tpu as pltpu

F32 = jnp.float32
BF16 = jnp.bfloat16

LANES = 128
SUBLANES = 8
VMEM_LIMIT = 56 * 1024 * 1024

ROW_TILE = 512
PROJ_COL_TILE = 512
MLP_HIDDEN_TILE = 1024
Q_BLOCK = 512
MOD_COL_TILE = 1024

N_HEADS = 8
QK_DIM = 64
V_DIM = 2 * QK_DIM
CHUNK = 64
ROT_DIM = QK_DIM // 4
ROPE_THETA = 500000.0
N_LRU_BLOCKS = 8
CONV_W = 4
LRU_C = 8.0
N_MOD = 6
EPS = 1e-6
LAM_INIT = 0.8 - 0.6 * math.exp(-0.3 * 0)
LOG2E = math.log2(math.e)


def _params(n_axes):
    return pltpu.CompilerParams(dimension_semantics=("arbitrary",) * n_axes,
                                vmem_limit_bytes=VMEM_LIMIT)


def _mod_body(c_ref, w_ref, b_ref, o_ref):
    c = c_ref[...]
    s = c * jax.nn.sigmoid(c)
    o_ref[...] = jnp.dot(s.astype(BF16), w_ref[...].astype(BF16),
                         preferred_element_type=F32) + b_ref[...]


def _modulation(c, w_ada, b_ada, tn=MOD_COL_TILE):
    rows, d = c.shape
    n = w_ada.shape[1]
    return pl.pallas_call(
        _mod_body,
        grid=(n // tn,),
        in_specs=[pl.BlockSpec((rows, d), lambda j: (0, 0)),
                  pl.BlockSpec((d, tn), lambda j: (0, j)),
                  pl.BlockSpec((1, tn), lambda j: (0, j))],
        out_specs=pl.BlockSpec((rows, tn), lambda j: (0, j)),
        out_shape=jax.ShapeDtypeStruct((rows, n), F32),
        compiler_params=_params(1),
        name="modulation",
    )(c, w_ada, b_ada.reshape(1, n))


def _mod_spec(bb, d, which, n_axes):
    if n_axes == 3:
        return pl.BlockSpec((bb, None, 1, d), lambda b, i, j: (b, which, 0, 0))
    return pl.BlockSpec((bb, None, 1, d), lambda b, i: (b, which, 0, 0))


def _rope_tables(pos):
    half = ROT_DIM // 2
    inv_freq = ROPE_THETA ** (-(jnp.arange(half, dtype=F32) * 2.0) / ROT_DIM)
    ang = pos.astype(F32)[:, None] * inv_freq[None, :]
    cos, sin = jnp.cos(ang), jnp.sin(ang)
    t = pos.shape[0]
    ones = jnp.ones((t, QK_DIM - ROT_DIM), F32)
    zeros_h = jnp.zeros((t, half), F32)
    zeros_r = jnp.zeros((t, QK_DIM - ROT_DIM), F32)
    c = jnp.concatenate([cos, cos, ones], axis=1)
    s1 = jnp.concatenate([zeros_h, sin, zeros_r], axis=1)
    s2 = jnp.concatenate([-sin, zeros_h, zeros_r], axis=1)
    rep = LANES // QK_DIM
    return tuple(jnp.tile(a, (1, rep)) for a in (c, s1, s2))


def _qk_norm_rope(acc, g, c, s1, s2):
    m, tn = acc.shape
    lane = lax.broadcasted_iota(jnp.int32, (m, LANES), 1)
    lo = lane < QK_DIM
    outs = []
    for t in range(tn // LANES):
        z = acc[:, t * LANES:(t + 1) * LANES]
        zz = z * z
        s_lo = jnp.sum(jnp.where(lo, zz, 0.0), axis=-1, keepdims=True)
        s_hi = jnp.sum(jnp.where(lo, 0.0, zz), axis=-1, keepdims=True)
        r = jnp.where(lo, lax.rsqrt(s_lo * (1.0 / QK_DIM) + EPS),
                      lax.rsqrt(s_hi * (1.0 / QK_DIM) + EPS))
        y = z * r * g
        y = (y * c + pltpu.roll(y, ROT_DIM // 2, 1) * s1
             + pltpu.roll(y, LANES - ROT_DIM // 2, 1) * s2)
        outs.append(y)
    return jnp.concatenate(outs, axis=1)


def _gelu_tanh(x):
    return 0.5 * x * (1.0 + jnp.tanh(math.sqrt(2.0 / math.pi) * (x + 0.044715 * (x * x * x))))


def _lru_tile(ext, gb, h_prev, cw, cb, wax_ref, ba, bx, lam):
    tt = ext.shape[0] - SUBLANES
    w = ext.shape[1]
    n_blocks = wax_ref.shape[0]
    blk = w // n_blocks
    groups = tt // SUBLANES
    first = SUBLANES - (CONV_W - 1)
    u = cb + cw[0:1, :] * ext[pl.ds(first, tt), :]
    for jj in range(1, CONV_W):
        u = u + cw[jj:jj + 1, :] * ext[pl.ds(first + jj, tt), :]

    ub = u.astype(BF16)
    ga, gx = [], []
    for n in range(n_blocks):
        res = jnp.dot(ub[:, n * blk:(n + 1) * blk], wax_ref[n], preferred_element_type=F32)
        ga.append(res[:, :blk])
        gx.append(res[:, blk:])
    r = jax.nn.sigmoid(jnp.concatenate(ga, axis=1) + ba)
    ig = jax.nn.sigmoid(jnp.concatenate(gx, axis=1) + bx)

    nl = -lam
    softplus = jnp.maximum(nl, 0.0) + jnp.log1p(jnp.exp(-jnp.abs(nl)))
    log_a = -LRU_C * r * softplus
    a = jnp.exp(log_a)
    b = jnp.sqrt(-jnp.tanh(log_a) * (a * a + 1.0)) * (ig * u)

    a3 = a.reshape(groups, SUBLANES, w)
    b3 = b.reshape(groups, SUBLANES, w)
    row = lax.broadcasted_iota(jnp.int32, (groups, SUBLANES, w), 1)
    for s in (1, 2, 4):
        keep = row >= s
        b3 = jnp.where(keep, a3 * pltpu.roll(b3, s, 1) + b3, b3)
        a3 = jnp.where(keep, a3 * pltpu.roll(a3, s, 1), a3)
    hs = []
    for g in range(groups):
        hg = a3[g] * h_prev + b3[g]
        hs.append(hg)
        h_prev = hg[SUBLANES - 1:SUBLANES, :]
    h = jnp.concatenate(hs, axis=0)
    return h * _gelu_tanh(gb), h_prev


def _inproj_body(x_ref, sh_ref, sc_ref, w_ref, gq_ref, gk_ref, rc_ref, rs1_ref, rs2_ref,
                 h0_ref, c0_ref, cw_ref, cb_ref, wax_ref, ba_ref, bx_ref, lam_ref,
                 q_ref, k_ref, v_ref, y_ref, hn_ref, cn_ref,
                 xn_scr, ext_scr, gb_scr, h_scr, *, bb, tt, tn):
    i = pl.program_id(1)
    m = bb * tt
    d = x_ref.shape[-1]
    x = x_ref[...]
    ms = jnp.mean(x * x, axis=-1, keepdims=True)
    xn = x * lax.rsqrt(ms + EPS) * (1.0 + sc_ref[...]) + sh_ref[...]
    xn_scr[...] = xn.reshape(m, d).astype(BF16)

    @pl.when(i == 0)
    def _():
        ext_scr[:, 0:SUBLANES, :] = c0_ref[...]
        h_scr[...] = h0_ref[...]

    sec_w = q_ref.shape[-1]
    tiles = sec_w // tn

    def project(s, h):
        col = s * sec_w + h * tn
        return jnp.dot(xn_scr[...], w_ref[:, col:col + tn], preferred_element_type=F32)

    for h in range(tiles):
        ext_scr[:, SUBLANES:, h * tn:(h + 1) * tn] = project(3, h).reshape(bb, tt, tn)
    for h in range(tiles):
        gb_scr[:, h * tn:(h + 1) * tn] = project(4, h)
    for b in range(bb):
        y, h_last = _lru_tile(ext_scr.at[b], gb_scr[b * tt:(b + 1) * tt, :], h_scr[b],
                              cw_ref[...], cb_ref[...], wax_ref, ba_ref[...], bx_ref[...],
                              lam_ref[...])
        y_ref[b] = y.astype(BF16)
        h_scr[b] = h_last
    ext_scr[:, 0:SUBLANES, :] = ext_scr[:, tt:tt + SUBLANES, :]
    hn_ref[...] = h_scr[...]
    cn_ref[...] = ext_scr[:, 0:SUBLANES, :]

    tabs = (rc_ref[...], rs1_ref[...], rs2_ref[...])
    if bb > 1:
        tabs = tuple(jnp.concatenate([a] * bb, axis=0) for a in tabs)
    for h in range(tiles):
        acc = _qk_norm_rope(project(0, h), gq_ref[...], *tabs) * (QK_DIM ** -0.5)
        q_ref[:, :, h * tn:(h + 1) * tn] = acc.astype(BF16).reshape(bb, tt, tn)
    for h in range(tiles):
        acc = _qk_norm_rope(project(1, h), gk_ref[...], *tabs)
        k_ref[:, :, h * tn:(h + 1) * tn] = acc.reshape(bb, tt, tn)
    for h in range(tiles):
        v_ref[:, :, h * tn:(h + 1) * tn] = project(2, h).reshape(bb, tt, tn)


def _in_projection(x, mod, w_in, g_q, g_k, ropes, h0, conv0, lru_weights, bb, tt,
                   tn=PROJ_COL_TILE):
    b, t, d = x.shape
    n = w_in.shape[1]
    sec_w = n // 5
    rc, rs1, rs2 = ropes
    conv_w, conv_b, w_ax, b_a, b_x, lru_lambda = lru_weights
    w = conv_w.shape[-1]
    blk = w // N_LRU_BLOCKS
    c0 = jnp.pad(conv0, ((0, 0), (SUBLANES - (CONV_W - 1), 0), (0, 0)))
    sec_spec = pl.BlockSpec((bb, tt, sec_w), lambda bi, i: (bi, i, 0))
    tab_spec = pl.BlockSpec((tt, LANES), lambda bi, i: (i, 0))
    g_spec = pl.BlockSpec((1, LANES), lambda bi, i: (0, 0))
    state_spec = lambda rows: pl.BlockSpec((bb, rows, w), lambda bi, i: (bi, 0, 0))
    vec_spec = lambda rows: pl.BlockSpec((rows, w), lambda bi, i: (0, 0))
    f32_out = jax.ShapeDtypeStruct((b, t, sec_w), F32)
    q, k, v, y, hn, cn = pl.pallas_call(
        functools.partial(_inproj_body, bb=bb, tt=tt, tn=tn),
        grid=(b // bb, t // tt),
        in_specs=[pl.BlockSpec((bb, tt, d), lambda bi, i: (bi, i, 0)),
                  _mod_spec(bb, d, 0, 2), _mod_spec(bb, d, 1, 2),
                  pl.BlockSpec((d, n), lambda bi, i: (0, 0), pipeline_mode=pl.Buffered(1)),
                  g_spec, g_spec, tab_spec, tab_spec, tab_spec,
                  state_spec(1), state_spec(SUBLANES), vec_spec(CONV_W), vec_spec(1),
                  pl.BlockSpec((N_LRU_BLOCKS, blk, 2 * blk), lambda bi, i: (0, 0, 0)),
                  vec_spec(1), vec_spec(1), vec_spec(1)],
        out_specs=[sec_spec, sec_spec, sec_spec, sec_spec, state_spec(1), state_spec(SUBLANES)],
        out_shape=[jax.ShapeDtypeStruct((b, t, sec_w), BF16), f32_out, f32_out,
                   jax.ShapeDtypeStruct((b, t, w), BF16),
                   jax.ShapeDtypeStruct((b, 1, w), F32),
                   jax.ShapeDtypeStruct((b, SUBLANES, w), F32)],
        scratch_shapes=[pltpu.VMEM((bb * tt, d), BF16),
                        pltpu.VMEM((bb, tt + SUBLANES, w), F32),
                        pltpu.VMEM((bb * tt, w), F32),
                        pltpu.VMEM((bb, 1, w), F32)],
        compiler_params=_params(2),
        name="in_projection",
    )(x, mod, mod, w_in, g_q, g_k, rc, rs1, rs2, h0.reshape(b, 1, w), c0,
      conv_w, conv_b.reshape(1, w), w_ax, b_a.reshape(1, w), b_x.reshape(1, w),
      lru_lambda.reshape(1, w))
    return q, k, v, y, hn[:, 0], cn[:, SUBLANES - (CONV_W - 1):]


def _inproj_stream_body(x_ref, sh_ref, sc_ref, w_ref, gq_ref, gk_ref, rc_ref, rs1_ref, rs2_ref,
                        h0_ref, c0_ref, cw_ref, cb_ref, wax_ref, ba_ref, bx_ref, lam_ref,
                        q_ref, k_ref, v_ref, y_ref, hn_ref, cn_ref, wbf_ref,
                        xn_scr, ext_scr, gb_scr, *, bb, tt, tn):
    j = pl.program_id(0)
    m = bb * tt
    d = x_ref.shape[-1]

    @pl.when(j == 0)
    def _():
        x = x_ref[...]
        ms = jnp.mean(x * x, axis=-1, keepdims=True)
        xn = x * lax.rsqrt(ms + EPS) * (1.0 + sc_ref[...]) + sh_ref[...]
        xn_scr[...] = xn.reshape(m, d).astype(BF16)
        ext_scr[:, 0:SUBLANES, :] = c0_ref[...]

    wt = w_ref[...].astype(BF16)
    wbf_ref[...] = wt
    acc = jnp.dot(xn_scr[...], wt, preferred_element_type=F32)

    sec_w = q_ref.shape[-1]
    tiles = sec_w // tn

    def tables():
        tabs = (rc_ref[...], rs1_ref[...], rs2_ref[...])
        return tuple(jnp.concatenate([a] * bb, axis=0) for a in tabs) if bb > 1 else tabs

    for c in range(5 * tiles):
        s, h = divmod(c, tiles)
        cols = slice(h * tn, (h + 1) * tn)

        @pl.when(j == c)
        def _(s=s, h=h, cols=cols):
            if s == 0:
                y = _qk_norm_rope(acc, gq_ref[...], *tables()) * (QK_DIM ** -0.5)
                q_ref[:, :, cols] = y.astype(BF16).reshape(bb, tt, tn)
            elif s == 1:
                k_ref[:, :, cols] = _qk_norm_rope(acc, gk_ref[...], *tables()).reshape(bb, tt, tn)
            elif s == 2:
                v_ref[:, :, cols] = acc.reshape(bb, tt, tn)
            elif s == 3:
                ext_scr[:, SUBLANES:, cols] = acc.reshape(bb, tt, tn)
            else:
                gb_scr[:, cols] = acc
                if h == tiles - 1:
                    for b in range(bb):
                        y, h_last = _lru_tile(ext_scr.at[b], gb_scr[b * tt:(b + 1) * tt, :],
                                              h0_ref[b], cw_ref[...], cb_ref[...], wax_ref,
                                              ba_ref[...], bx_ref[...], lam_ref[...])
                        y_ref[b] = y.astype(BF16)
                        hn_ref[b] = h_last
                    cn_ref[...] = ext_scr[:, tt:tt + SUBLANES, :]


def _in_projection_streamed(x, mod, w_in_f32, g_q, g_k, ropes, h0, conv0, lru_weights,
                            tn=PROJ_COL_TILE):
    b, t, d = x.shape
    n = w_in_f32.shape[1]
    sec_w = n // 5
    rc, rs1, rs2 = ropes
    conv_w, conv_b, w_ax, b_a, b_x, lru_lambda = lru_weights
    w = conv_w.shape[-1]
    blk = w // N_LRU_BLOCKS
    c0 = jnp.pad(conv0, ((0, 0), (SUBLANES - (CONV_W - 1), 0), (0, 0)))
    whole = lambda *shape: pl.BlockSpec(shape, lambda j: (0,) * len(shape))
    mod_spec = lambda which: pl.BlockSpec((b, None, 1, d), lambda j: (0, which, 0, 0))
    col_tile = pl.BlockSpec((d, tn), lambda j: (0, j))
    f32_out = jax.ShapeDtypeStruct((b, t, sec_w), F32)
    q, k, v, y, hn, cn, w_bf = pl.pallas_call(
        functools.partial(_inproj_stream_body, bb=b, tt=t, tn=tn),
        grid=(n // tn,),
        in_specs=[whole(b, t, d), mod_spec(0), mod_spec(1), col_tile,
                  whole(1, LANES), whole(1, LANES),
                  whole(t, LANES), whole(t, LANES), whole(t, LANES),
                  whole(b, 1, w), whole(b, SUBLANES, w), whole(CONV_W, w), whole(1, w),
                  whole(N_LRU_BLOCKS, blk, 2 * blk), whole(1, w), whole(1, w), whole(1, w)],
        out_specs=[whole(b, t, sec_w), whole(b, t, sec_w), whole(b, t, sec_w), whole(b, t, w),
                   whole(b, 1, w), whole(b, SUBLANES, w), col_tile],
        out_shape=[jax.ShapeDtypeStruct((b, t, sec_w), BF16), f32_out, f32_out,
                   jax.ShapeDtypeStruct((b, t, w), BF16),
                   jax.ShapeDtypeStruct((b, 1, w), F32),
                   jax.ShapeDtypeStruct((b, SUBLANES, w), F32),
                   jax.ShapeDtypeStruct((d, n), BF16)],
        scratch_shapes=[pltpu.VMEM((b * t, d), BF16),
                        pltpu.VMEM((b, t + SUBLANES, w), F32),
                        pltpu.VMEM((b * t, w), F32)],
        compiler_params=_params(1),
        name="in_projection_streamed",
    )(x, mod, mod, w_in_f32, g_q, g_k, rc, rs1, rs2, h0.reshape(b, 1, w), c0,
      conv_w, conv_b.reshape(1, w), w_ax, b_a.reshape(1, w), b_x.reshape(1, w),
      lru_lambda.reshape(1, w))
    return (q, k, v, y, hn[:, 0], cn[:, SUBLANES - (CONV_W - 1):]), w_bf


def _lambda_value(lq1_ref, lk1_ref, lq2_ref, lk2_ref):
    s1 = jnp.sum(lq1_ref[...] * lk1_ref[...], axis=-1, keepdims=True)
    s2 = jnp.sum(lq2_ref[...] * lk2_ref[...], axis=-1, keepdims=True)
    return jnp.exp(s1) - jnp.exp(s2) + LAM_INIT


def _stack_components(q):
    lane = lax.broadcasted_iota(jnp.int32, q.shape, 1)
    zero = jnp.zeros_like(q)
    return jnp.concatenate([jnp.where(lane < QK_DIM, q, zero),
                            jnp.where(lane < QK_DIM, zero, q)], axis=0)


def _attn_prompt_body(lq1_ref, lk1_ref, lq2_ref, lk2_ref, g_ref, q_ref, k_ref, v_ref,
                      wo_ref, wu_ref, wd_ref, o_ref, wo_bf_ref, wu_bf_ref, wd_bf_ref,
                      kb_scr, vt_scr, s_scr, *, tq):
    for src, dst in ((wo_ref, wo_bf_ref), (wu_ref, wu_bf_ref), (wd_ref, wd_bf_ref)):
        dst[...] = src[...].astype(BF16)
    t = q_ref.shape[1]
    nq = t // tq
    kb_scr[...] = (k_ref[0] * LOG2E).astype(BF16)
    vt_scr[...] = v_ref[0].T.astype(BF16)
    lam = _lambda_value(lq1_ref, lk1_ref, lq2_ref, lk2_ref)
    gain = g_ref[...] * (1.0 - LAM_INIT)

    key_chunk = lax.broadcasted_iota(jnp.int32, (tq, 2 * tq), 0) // CHUNK
    qcol = lax.broadcasted_iota(jnp.int32, (tq, 2 * tq), 1)
    q_chunk = jnp.where(qcol >= tq, qcol - tq, qcol) // CHUNK
    visible = key_chunk <= q_chunk

    def fold(x):
        return x.reshape(tq // SUBLANES, SUBLANES, x.shape[-1])

    def stacked_q(i):
        return _stack_components(q_ref[0, i * tq:(i + 1) * tq, :])

    def score_chunk(i, c, qs, m8):
        sc = lax.dot_general(kb_scr[c * tq:(c + 1) * tq, :], qs, (((1,), (1,)), ((), ())),
                             preferred_element_type=F32)
        if c == i:
            sc = jnp.where(visible, sc, -1e30)
        s_scr[i % 2, c * tq:(c + 1) * tq, :] = sc
        return jnp.maximum(m8, jnp.max(fold(sc), axis=0))

    def value_chunk(i, c, m, l8, ot):
        p = jnp.exp2(s_scr[i % 2, c * tq:(c + 1) * tq, :] - m)
        ot = ot + jnp.dot(vt_scr[:, c * tq:(c + 1) * tq], p.astype(BF16),
                          preferred_element_type=F32)
        return l8 + jnp.sum(fold(p), axis=0), ot

    neg_inf = jnp.full((SUBLANES, 2 * tq), -jnp.inf, F32)
    qs = stacked_q(0)
    m8 = score_chunk(0, 0, qs, neg_inf)
    for i in range(nq):
        m = jnp.max(m8, axis=0, keepdims=True)
        l8 = jnp.zeros((SUBLANES, 2 * tq), F32)
        ot = jnp.zeros((V_DIM, 2 * tq), F32)
        if i + 1 < nq:
            qs, m8 = stacked_q(i + 1), neg_inf
        for c in range(i + 2):
            if c <= i:
                l8, ot = value_chunk(i, c, m, l8, ot)
            if i + 1 < nq:
                m8 = score_chunk(i + 1, c, qs, m8)
        ot = ot / jnp.sum(l8, axis=0, keepdims=True)
        o = ot[:, :tq] - lam * ot[:, tq:]
        ms = jnp.mean(o * o, axis=0, keepdims=True)
        o = o * lax.rsqrt(ms + EPS) * gain
        o_ref[0, i * tq:(i + 1) * tq, :] = o.T.astype(BF16)


def _attention_prompt(q, k, v, lams, g_subln, f32_weights, tq):
    b, t, aw = q.shape
    n_steps = b * N_HEADS
    head = pl.BlockSpec((1, t, V_DIM), lambda bi, h: (bi, 0, h))
    lam_spec = pl.BlockSpec((1, QK_DIM), lambda bi, h: (0, 0))

    def slice_spec(wt):
        rows = wt.shape[0] // n_steps
        assert rows * n_steps == wt.shape[0] and rows % (2 * SUBLANES) == 0
        return pl.BlockSpec((rows, wt.shape[1]), lambda bi, h: (bi * N_HEADS + h, 0))

    cast_specs = [slice_spec(wt) for wt in f32_weights]
    out, *casts = pl.pallas_call(
        functools.partial(_attn_prompt_body, tq=tq),
        grid=(b, N_HEADS),
        in_specs=[lam_spec] * 4 + [pl.BlockSpec((V_DIM, 1), lambda bi, h: (0, 0)),
                                   head, head, head] + cast_specs,
        out_specs=[head] + cast_specs,
        out_shape=[jax.ShapeDtypeStruct((b, t, aw), BF16)]
        + [jax.ShapeDtypeStruct(wt.shape, BF16) for wt in f32_weights],
        scratch_shapes=[pltpu.VMEM((t, V_DIM), BF16), pltpu.VMEM((V_DIM, t), BF16),
                        pltpu.VMEM((2, t, 2 * tq), F32)],
        compiler_params=_params(2),
        name="attention_prompt",
    )(*lams, g_subln.reshape(V_DIM, 1), q, k, v, *f32_weights)
    return out, casts


def _attn_sample_body(lq1_ref, lk1_ref, lq2_ref, lk2_ref, g_ref, q_ref, kt_ref, pv_ref,
                      nk_ref, nv_ref, o_ref, *, tq):
    lam = _lambda_value(lq1_ref, lk1_ref, lq2_ref, lk2_ref)
    gain = g_ref[...] * (1.0 - LAM_INIT)
    for h in range(N_HEADS):
        cols = slice(h * V_DIM, (h + 1) * V_DIM)
        qs = _stack_components(q_ref[0, :, cols])
        s_past = jnp.dot(qs, kt_ref[0, cols, :].astype(BF16), preferred_element_type=F32)
        s_new = lax.dot_general(qs, nk_ref[0, :, cols].astype(BF16), (((1,), (1,)), ((), ())),
                                preferred_element_type=F32)
        m = jnp.maximum(jnp.max(s_past, axis=-1, keepdims=True),
                        jnp.max(s_new, axis=-1, keepdims=True))
        p_past = jnp.exp(s_past - m)
        p_new = jnp.exp(s_new - m)
        l = jnp.sum(p_past, axis=-1, keepdims=True) + jnp.sum(p_new, axis=-1, keepdims=True)
        o = jnp.dot(p_past.astype(BF16), pv_ref[:, h, :].astype(BF16),
                    preferred_element_type=F32)
        o = o + jnp.dot(p_new.astype(BF16), nv_ref[0, :, cols].astype(BF16),
                        preferred_element_type=F32)
        o = o / l
        o = o[:tq] - lam * o[tq:]
        ms = jnp.mean(o * o, axis=-1, keepdims=True)
        o_ref[0, :, cols] = (o * lax.rsqrt(ms + EPS) * gain).astype(BF16)


def _attention_sample(q, past_kt, past_v, new_k, new_v, lams, g_subln):
    b, t, aw = q.shape
    past = past_kt.shape[-1]
    assert past % CHUNK == 0 and t <= CHUNK
    row_new = pl.BlockSpec((1, t, aw), lambda bi: (bi, 0, 0))
    vec = lambda width: pl.BlockSpec((1, width), lambda bi: (0, 0))
    return pl.pallas_call(
        functools.partial(_attn_sample_body, tq=t),
        grid=(b,),
        in_specs=[vec(QK_DIM)] * 4 + [
            vec(V_DIM), row_new,
            pl.BlockSpec((1, aw, past), lambda bi: (bi, 0, 0)),
            pl.BlockSpec((None, past, N_HEADS, V_DIM), lambda bi: (bi, 0, 0, 0)),
            row_new, row_new],
        out_specs=row_new,
        out_shape=jax.ShapeDtypeStruct((b, t, aw), BF16),
        compiler_params=_params(1),
        name="attention_sample",
    )(*lams, g_subln, q, past_kt, past_v, new_k, new_v)


def _mlp_body(x_ref, o_ref, y_ref, gt1_ref, sh2_ref, sc2_ref, gt2_ref, wo_ref, wu_ref, wd_ref,
              out_ref, xn_scr, acc_scr, *, bb, tt):
    f = pl.program_id(2)
    m = bb * tt
    d = x_ref.shape[-1]
    aw = o_ref.shape[-1]

    @pl.when(f == 0)
    def _():
        mix = jnp.dot(o_ref[...].reshape(m, aw), wo_ref[0:aw, :], preferred_element_type=F32)
        mix = mix + jnp.dot(y_ref[...].reshape(m, y_ref.shape[-1]), wo_ref[aw:, :],
                            preferred_element_type=F32)
        x1 = x_ref[...] + gt1_ref[...] * mix.reshape(bb, tt, d)
        out_ref[...] = x1
        ms = jnp.mean(x1 * x1, axis=-1, keepdims=True)
        xn = x1 * lax.rsqrt(ms + EPS) * (1.0 + sc2_ref[...]) + sh2_ref[...]
        xn_scr[...] = xn.reshape(m, d).astype(BF16)
        acc_scr[...] = jnp.zeros(acc_scr.shape, F32)

    hid = jnp.maximum(jnp.dot(xn_scr[...], wu_ref[...], preferred_element_type=F32), 0.0)
    acc_scr[...] += jnp.dot((hid * hid).astype(BF16), wd_ref[...], preferred_element_type=F32)

    @pl.when(f == pl.num_programs(2) - 1)
    def _():
        out_ref[...] = out_ref[...] + gt2_ref[...] * acc_scr[...].reshape(bb, tt, d)


def _outproj_mlp(x, o, y, mod, w_out, w_up, w_down, bb, tt, tf=MLP_HIDDEN_TILE):
    b, t, d = x.shape
    aw, lw = o.shape[-1], y.shape[-1]
    dff = w_up.shape[1]
    row = lambda width: pl.BlockSpec((bb, tt, width), lambda bi, i, f: (bi, i, 0))
    return pl.pallas_call(
        functools.partial(_mlp_body, bb=bb, tt=tt),
        grid=(b // bb, t // tt, dff // tf),
        in_specs=[row(d), row(aw), row(lw),
                  _mod_spec(bb, d, 2, 3), _mod_spec(bb, d, 3, 3),
                  _mod_spec(bb, d, 4, 3), _mod_spec(bb, d, 5, 3),
                  pl.BlockSpec((aw + lw, d), lambda bi, i, f: (0, 0),
                               pipeline_mode=pl.Buffered(1)),
                  pl.BlockSpec((d, tf), lambda bi, i, f: (0, f)),
                  pl.BlockSpec((tf, d), lambda bi, i, f: (f, 0))],
        out_specs=row(d),
        out_shape=jax.ShapeDtypeStruct((b, t, d), F32),
        scratch_shapes=[pltpu.VMEM((bb * tt, d), BF16), pltpu.VMEM((bb * tt, d), F32)],
        compiler_params=_params(3),
        name="outproj_mlp",
    )(x, o, y, mod, mod, mod, mod, w_out, w_up, w_down)


def kernel(x_prompt, x_sample, c_prompt, c_sample, cache_k, cache_v, state_lru_h, state_conv,
           w_ada, b_ada, w_in, g_q, g_k, lambda_q1, lambda_k1, lambda_q2, lambda_k2, g_subln,
           conv_w, conv_b, w_gate_a, b_gate_a, w_gate_x, b_gate_x, lru_lambda,
           w_out, w_up, w_down):
    bp, tp, d = x_prompt.shape
    bs, ts, _ = x_sample.shape
    past = cache_k.shape[2]
    aw = N_HEADS * V_DIM
    lw = conv_w.shape[-1]

    c_all = jnp.concatenate([c_prompt, c_sample], axis=0)
    mod = _modulation(c_all, w_ada[0], b_ada[0]).reshape(bp + bs, N_MOD, 1, d)
    mod_p, mod_s = mod[:bp], mod[bp:]

    rep = LANES // QK_DIM
    gq = jnp.tile(g_q[0], rep).reshape(1, LANES)
    gk = jnp.tile(g_k[0], rep).reshape(1, LANES)
    lru_weights = (conv_w[0], conv_b[0],
                   jnp.concatenate([w_gate_a[0], w_gate_x[0]], axis=-1).astype(BF16),
                   b_gate_a[0].reshape(-1), b_gate_x[0].reshape(-1), lru_lambda[0])
    lams = tuple(a[0].reshape(1, QK_DIM) for a in (lambda_q1, lambda_k1, lambda_q2, lambda_k2))
    g_sub = g_subln[0].reshape(1, V_DIM)

    (qs, ks, vs, ys, hs, cs), w_in_bf = _in_projection_streamed(
        x_sample, mod_s, w_in[0], gq, gk, _rope_tables(past + jnp.arange(ts)),
        state_lru_h[0], state_conv[0], lru_weights)
    past_kt = jnp.transpose(cache_k[0], (0, 2, 3, 4, 1)).reshape(bs, aw, past)
    osamp = _attention_sample(qs, past_kt, cache_v[0], ks, vs, lams, g_sub)

    tt_p = min(tp, ROW_TILE)
    qp, kp, vp, yp, hp, cp = _in_projection(
        x_prompt, mod_p, w_in_bf, gq, gk, _rope_tables(jnp.arange(tp)),
        jnp.zeros((bp, lw), F32), jnp.zeros((bp, CONV_W - 1, lw), F32), lru_weights, 1, tt_p)
    op, (wo, wu, wd) = _attention_prompt(qp, kp, vp, lams, g_sub,
                                         (w_out[0], w_up[0], w_down[0]), tq=min(tp, Q_BLOCK))
    out_p = _outproj_mlp(x_prompt, op, yp, mod_p, wo, wu, wd, 1, tt_p)
    out_s = _outproj_mlp(x_sample, osamp, ys, mod_s, wo, wu, wd, bs, ts)

    return (out_p, out_s,
            kp.reshape(1, bp, tp, N_HEADS, 2, QK_DIM), vp.reshape(1, bp, tp, N_HEADS, V_DIM),
            hp[None], cp[None],
            ks.reshape(1, bs, ts, N_HEADS, 2, QK_DIM), vs.reshape(1, bs, ts, N_HEADS, V_DIM),
            hs[None], cs[None])
```

```python
import functools
import math

import jax
import jax.numpy as jnp
from jax import lax
from jax.experimental import pallas as pl
from jax.experimental.pallas import tpu as pltpu

F32 = jnp.float32
BF16 = jnp.bfloat16

LANES = 128
SUBLANES = 8
VMEM_LIMIT = 56 * 1024 * 1024

ROW_TILE = 512
PROJ_COL_TILE = 512
MLP_HIDDEN_TILE = 1024
Q_BLOCK = 512
MOD_COL_TILE = 1024

N_HEADS = 8
QK_DIM = 64
V_DIM = 2 * QK_DIM
CHUNK = 64
ROT_DIM = QK_DIM // 4
ROPE_THETA = 500000.0
N_LRU_BLOCKS = 8
CONV_W = 4
LRU_C = 8.0
N_MOD = 6
EPS = 1e-6
LAM_INIT = 0.8 - 0.6 * math.exp(-0.3 * 0)
LOG2E = math.log2(math.e)


def _params(n_axes):
    return pltpu.CompilerParams(dimension_semantics=("arbitrary",) * n_axes,
                                vmem_limit_bytes=VMEM_LIMIT)


def _mod_body(c_ref, w_ref, b_ref, o_ref):
    c = c_ref[...]
    s = c * jax.nn.sigmoid(c)
    o_ref[...] = jnp.dot(s.astype(BF16), w_ref[...].astype(BF16),
                         preferred_element_type=F32) + b_ref[...]


def _modulation(c, w_ada, b_ada, tn=MOD_COL_TILE):
    rows, d = c.shape
    n = w_ada.shape[1]
    return pl.pallas_call(
        _mod_body,
        grid=(n // tn,),
        in_specs=[pl.BlockSpec((rows, d), lambda j: (0, 0)),
                  pl.BlockSpec((d, tn), lambda j: (0, j)),
                  pl.BlockSpec((1, tn), lambda j: (0, j))],
        out_specs=pl.BlockSpec((rows, tn), lambda j: (0, j)),
        out_shape=jax.ShapeDtypeStruct((rows, n), F32),
        compiler_params=_params(1),
        name="modulation",
    )(c, w_ada, b_ada.reshape(1, n))


def _mod_spec(bb, d, which, n_axes):
    if n_axes == 3:
        return pl.BlockSpec((bb, None, 1, d), lambda b, i, j: (b, which, 0, 0))
    return pl.BlockSpec((bb, None, 1, d), lambda b, i: (b, which, 0, 0))


def _rope_tables(pos):
    half = ROT_DIM // 2
    inv_freq = ROPE_THETA ** (-(jnp.arange(half, dtype=F32) * 2.0) / ROT_DIM)
    ang = pos.astype(F32)[:, None] * inv_freq[None, :]
    cos, sin = jnp.cos(ang), jnp.sin(ang)
    t = pos.shape[0]
    ones = jnp.ones((t, QK_DIM - ROT_DIM), F32)
    zeros_h = jnp.zeros((t, half), F32)
    zeros_r = jnp.zeros((t, QK_DIM - ROT_DIM), F32)
    c = jnp.concatenate([cos, cos, ones], axis=1)
    s1 = jnp.concatenate([zeros_h, sin, zeros_r], axis=1)
    s2 = jnp.concatenate([-sin, zeros_h, zeros_r], axis=1)
    rep = LANES // QK_DIM
    return tuple(jnp.tile(a, (1, rep)) for a in (c, s1, s2))


def _qk_norm_rope(acc, g, c, s1, s2):
    m, tn = acc.shape
    lane = lax.broadcasted_iota(jnp.int32, (m, LANES), 1)
    lo = lane < QK_DIM
    outs = []
    for t in range(tn // LANES):
        z = acc[:, t * LANES:(t + 1) * LANES]
        zz = z * z
        s_lo = jnp.sum(jnp.where(lo, zz, 0.0), axis=-1, keepdims=True)
        s_hi = jnp.sum(jnp.where(lo, 0.0, zz), axis=-1, keepdims=True)
        r = jnp.where(lo, lax.rsqrt(s_lo * (1.0 / QK_DIM) + EPS),
                      lax.rsqrt(s_hi * (1.0 / QK_DIM) + EPS))
        y = z * r * g
        y = (y * c + pltpu.roll(y, ROT_DIM // 2, 1) * s1
             + pltpu.roll(y, LANES - ROT_DIM // 2, 1) * s2)
        outs.append(y)
    return jnp.concatenate(outs, axis=1)


def _gelu_tanh(x):
    return 0.5 * x * (1.0 + jnp.tanh(math.sqrt(2.0 / math.pi) * (x + 0.044715 * (x * x * x))))


def _lru_tile(ext, gb, h_prev, cw, cb, wax_ref, ba, bx, lam):
    tt = ext.shape[0] - SUBLANES
    w = ext.shape[1]
    n_blocks = wax_ref.shape[0]
    blk = w // n_blocks
    groups = tt // SUBLANES
    first = SUBLANES - (CONV_W - 1)
    u = cb + cw[0:1, :] * ext[pl.ds(first, tt), :]
    for jj in range(1, CONV_W):
        u = u + cw[jj:jj + 1, :] * ext[pl.ds(first + jj, tt), :]

    ub = u.astype(BF16)
    ga, gx = [], []
    for n in range(n_blocks):
        res = jnp.dot(ub[:, n * blk:(n + 1) * blk], wax_ref[n], preferred_element_type=F32)
        ga.append(res[:, :blk])
        gx.append(res[:, blk:])
    r = jax.nn.sigmoid(jnp.concatenate(ga, axis=1) + ba)
    ig = jax.nn.sigmoid(jnp.concatenate(gx, axis=1) + bx)

    nl = -lam
    softplus = jnp.maximum(nl, 0.0) + jnp.log1p(jnp.exp(-jnp.abs(nl)))
    log_a = -LRU_C * r * softplus
    a = jnp.exp(log_a)
    b = jnp.sqrt(-jnp.tanh(log_a) * (a * a + 1.0)) * (ig * u)

    a3 = a.reshape(groups, SUBLANES, w)
    b3 = b.reshape(groups, SUBLANES, w)
    row = lax.broadcasted_iota(jnp.int32, (groups, SUBLANES, w), 1)
    for s in (1, 2, 4):
        keep = row >= s
        b3 = jnp.where(keep, a3 * pltpu.roll(b3, s, 1) + b3, b3)
        a3 = jnp.where(keep, a3 * pltpu.roll(a3, s, 1), a3)
    hs = []
    for g in range(groups):
        hg = a3[g] * h_prev + b3[g]
        hs.append(hg)
        h_prev = hg[SUBLANES - 1:SUBLANES, :]
    h = jnp.concatenate(hs, axis=0)
    return h * _gelu_tanh(gb), h_prev


def _inproj_body(x_ref, sh_ref, sc_ref, w_ref, gq_ref, gk_ref, rc_ref, rs1_ref, rs2_ref,
                 h0_ref, c0_ref, cw_ref, cb_ref, wax_ref, ba_ref, bx_ref, lam_ref,
                 q_ref, k_ref, v_ref, y_ref, hn_ref, cn_ref,
                 xn_scr, ext_scr, gb_scr, h_scr, *, bb, tt, tn):
    i = pl.program_id(1)
    m = bb * tt
    d = x_ref.shape[-1]
    x = x_ref[...]
    ms = jnp.mean(x * x, axis=-1, keepdims=True)
    xn = x * lax.rsqrt(ms + EPS) * (1.0 + sc_ref[...]) + sh_ref[...]
    xn_scr[...] = xn.reshape(m, d).astype(BF16)

    @pl.when(i == 0)
    def _():
        ext_scr[:, 0:SUBLANES, :] = c0_ref[...]
        h_scr[...] = h0_ref[...]

    sec_w = q_ref.shape[-1]
    tiles = sec_w // tn

    def project(s, h):
        col = s * sec_w + h * tn
        return jnp.dot(xn_scr[...], w_ref[:, col:col + tn], preferred_element_type=F32)

    for h in range(tiles):
        ext_scr[:, SUBLANES:, h * tn:(h + 1) * tn] = project(3, h).reshape(bb, tt, tn)
    for h in range(tiles):
        gb_scr[:, h * tn:(h + 1) * tn] = project(4, h)
    for b in range(bb):
        y, h_last = _lru_tile(ext_scr.at[b], gb_scr[b * tt:(b + 1) * tt, :], h_scr[b],
                              cw_ref[...], cb_ref[...], wax_ref, ba_ref[...], bx_ref[...],
                              lam_ref[...])
        y_ref[b] = y.astype(BF16)
        h_scr[b] = h_last
    ext_scr[:, 0:SUBLANES, :] = ext_scr[:, tt:tt + SUBLANES, :]
    hn_ref[...] = h_scr[...]
    cn_ref[...] = ext_scr[:, 0:SUBLANES, :]

    tabs = (rc_ref[...], rs1_ref[...], rs2_ref[...])
    if bb > 1:
        tabs = tuple(jnp.concatenate([a] * bb, axis=0) for a in tabs)
    for h in range(tiles):
        acc = _qk_norm_rope(project(0, h), gq_ref[...], *tabs) * (QK_DIM ** -0.5)
        q_ref[:, :, h * tn:(h + 1) * tn] = acc.astype(BF16).reshape(bb, tt, tn)
    for h in range(tiles):
        acc = _qk_norm_rope(project(1, h), gk_ref[...], *tabs)
        k_ref[:, :, h * tn:(h + 1) * tn] = acc.reshape(bb, tt, tn)
    for h in range(tiles):
        v_ref[:, :, h * tn:(h + 1) * tn] = project(2, h).reshape(bb, tt, tn)


def _in_projection(x, mod, w_in, g_q, g_k, ropes, h0, conv0, lru_weights, bb, tt,
                   tn=PROJ_COL_TILE):
    b, t, d = x.shape
    n = w_in.shape[1]
    sec_w = n // 5
    rc, rs1, rs2 = ropes
    conv_w, conv_b, w_ax, b_a, b_x, lru_lambda = lru_weights
    w = conv_w.shape[-1]
    blk = w // N_LRU_BLOCKS
    c0 = jnp.pad(conv0, ((0, 0), (SUBLANES - (CONV_W - 1), 0), (0, 0)))
    sec_spec = pl.BlockSpec((bb, tt, sec_w), lambda bi, i: (bi, i, 0))
    tab_spec = pl.BlockSpec((tt, LANES), lambda bi, i: (i, 0))
    g_spec = pl.BlockSpec((1, LANES), lambda bi, i: (0, 0))
    state_spec = lambda rows: pl.BlockSpec((bb, rows, w), lambda bi, i: (bi, 0, 0))
    vec_spec = lambda rows: pl.BlockSpec((rows, w), lambda bi, i: (0, 0))
    f32_out = jax.ShapeDtypeStruct((b, t, sec_w), F32)
    q, k, v, y, hn, cn = pl.pallas_call(
        functools.partial(_inproj_body, bb=bb, tt=tt, tn=tn),
        grid=(b // bb, t // tt),
        in_specs=[pl.BlockSpec((bb, tt, d), lambda bi, i: (bi, i, 0)),
                  _mod_spec(bb, d, 0, 2), _mod_spec(bb, d, 1, 2),
                  pl.BlockSpec((d, n), lambda bi, i: (0, 0), pipeline_mode=pl.Buffered(1)),
                  g_spec, g_spec, tab_spec, tab_spec, tab_spec,
                  state_spec(1), state_spec(SUBLANES), vec_spec(CONV_W), vec_spec(1),
                  pl.BlockSpec((N_LRU_BLOCKS, blk, 2 * blk), lambda bi, i: (0, 0, 0)),
                  vec_spec(1), vec_spec(1), vec_spec(1)],
        out_specs=[sec_spec, sec_spec, sec_spec, sec_spec, state_spec(1), state_spec(SUBLANES)],
        out_shape=[jax.ShapeDtypeStruct((b, t, sec_w), BF16), f32_out, f32_out,
                   jax.ShapeDtypeStruct((b, t, w), BF16),
                   jax.ShapeDtypeStruct((b, 1, w), F32),
                   jax.ShapeDtypeStruct((b, SUBLANES, w), F32)],
        scratch_shapes=[pltpu.VMEM((bb * tt, d), BF16),
                        pltpu.VMEM((bb, tt + SUBLANES, w), F32),
                        pltpu.VMEM((bb * tt, w), F32),
                        pltpu.VMEM((bb, 1, w), F32)],
        compiler_params=_params(2),
        name="in_projection",
    )(x, mod, mod, w_in, g_q, g_k, rc, rs1, rs2, h0.reshape(b, 1, w), c0,
      conv_w, conv_b.reshape(1, w), w_ax, b_a.reshape(1, w), b_x.reshape(1, w),
      lru_lambda.reshape(1, w))
    return q, k, v, y, hn[:, 0], cn[:, SUBLANES - (CONV_W - 1):]


def _inproj_stream_body(x_ref, sh_ref, sc_ref, w_ref, gq_ref, gk_ref, rc_ref, rs1_ref, rs2_ref,
                        h0_ref, c0_ref, cw_ref, cb_ref, wax_ref, ba_ref, bx_ref, lam_ref,
                        q_ref, k_ref, v_ref, y_ref, hn_ref, cn_ref, wbf_ref,
                        xn_scr, ext_scr, gb_scr, *, bb, tt, tn):
    j = pl.program_id(0)
    m = bb * tt
    d = x_ref.shape[-1]

    @pl.when(j == 0)
    def _():
        x = x_ref[...]
        ms = jnp.mean(x * x, axis=-1, keepdims=True)
        xn = x * lax.rsqrt(ms + EPS) * (1.0 + sc_ref[...]) + sh_ref[...]
        xn_scr[...] = xn.reshape(m, d).astype(BF16)
        ext_scr[:, 0:SUBLANES, :] = c0_ref[...]

    wt = w_ref[...].astype(BF16)
    wbf_ref[...] = wt
    acc = jnp.dot(xn_scr[...], wt, preferred_element_type=F32)

    sec_w = q_ref.shape[-1]
    tiles = sec_w // tn

    def tables():
        tabs = (rc_ref[...], rs1_ref[...], rs2_ref[...])
        return tuple(jnp.concatenate([a] * bb, axis=0) for a in tabs) if bb > 1 else tabs

    for c in range(5 * tiles):
        s, h = divmod(c, tiles)
        cols = slice(h * tn, (h + 1) * tn)

        @pl.when(j == c)
        def _(s=s, h=h, cols=cols):
            if s == 0:
                y = _qk_norm_rope(acc, gq_ref[...], *tables()) * (QK_DIM ** -0.5)
                q_ref[:, :, cols] = y.astype(BF16).reshape(bb, tt, tn)
            elif s == 1:
                k_ref[:, :, cols] = _qk_norm_rope(acc, gk_ref[...], *tables()).reshape(bb, tt, tn)
            elif s == 2:
                v_ref[:, :, cols] = acc.reshape(bb, tt, tn)
            elif s == 3:
                ext_scr[:, SUBLANES:, cols] = acc.reshape(bb, tt, tn)
            else:
                gb_scr[:, cols] = acc
                if h == tiles - 1:
                    for b in range(bb):
                        y, h_last = _lru_tile(ext_scr.at[b], gb_scr[b * tt:(b + 1) * tt, :],
                                              h0_ref[b], cw_ref[...], cb_ref[...], wax_ref,
                                              ba_ref[...], bx_ref[...], lam_ref[...])
                        y_ref[b] = y.astype(BF16)
                        hn_ref[b] = h_last
                    cn_ref[...] = ext_scr[:, tt:tt + SUBLANES, :]


def _in_projection_streamed(x, mod, w_in_f32, g_q, g_k, ropes, h0, conv0, lru_weights,
                            tn=PROJ_COL_TILE):
    b, t, d = x.shape
    n = w_in_f32.shape[1]
    sec_w = n // 5
    rc, rs1, rs2 = ropes
    conv_w, conv_b, w_ax, b_a, b_x, lru_lambda = lru_weights
    w = conv_w.shape[-1]
    blk = w // N_LRU_BLOCKS
    c0 = jnp.pad(conv0, ((0, 0), (SUBLANES - (CONV_W - 1), 0), (0, 0)))
    whole = lambda *shape: pl.BlockSpec(shape, lambda j: (0,) * len(shape))
    mod_spec = lambda which: pl.BlockSpec((b, None, 1, d), lambda j: (0, which, 0, 0))
    col_tile = pl.BlockSpec((d, tn), lambda j: (0, j))
    f32_out = jax.ShapeDtypeStruct((b, t, sec_w), F32)
    q, k, v, y, hn, cn, w_bf = pl.pallas_call(
        functools.partial(_inproj_stream_body, bb=b, tt=t, tn=tn),
        grid=(n // tn,),
        in_specs=[whole(b, t, d), mod_spec(0), mod_spec(1), col_tile,
                  whole(1, LANES), whole(1, LANES),
                  whole(t, LANES), whole(t, LANES), whole(t, LANES),
                  whole(b, 1, w), whole(b, SUBLANES, w), whole(CONV_W, w), whole(1, w),
                  whole(N_LRU_BLOCKS, blk, 2 * blk), whole(1, w), whole(1, w), whole(1, w)],
        out_specs=[whole(b, t, sec_w), whole(b, t, sec_w), whole(b, t, sec_w), whole(b, t, w),
                   whole(b, 1, w), whole(b, SUBLANES, w), col_tile],
        out_shape=[jax.ShapeDtypeStruct((b, t, sec_w), BF16), f32_out, f32_out,
                   jax.ShapeDtypeStruct((b, t, w), BF16),
                   jax.ShapeDtypeStruct((b, 1, w), F32),
                   jax.ShapeDtypeStruct((b, SUBLANES, w), F32),
                   jax.ShapeDtypeStruct((d, n), BF16)],
        scratch_shapes=[pltpu.VMEM((b * t, d), BF16),
                        pltpu.VMEM((b, t + SUBLANES, w), F32),
                        pltpu.VMEM((b * t, w), F32)],
        compiler_params=_params(1),
        name="in_projection_streamed",
    )(x, mod, mod, w_in_f32, g_q, g_k, rc, rs1, rs2, h0.reshape(b, 1, w), c0,
      conv_w, conv_b.reshape(1, w), w_ax, b_a.reshape(1, w), b_x.reshape(1, w),
      lru_lambda.reshape(1, w))
    return (q, k, v, y, hn[:, 0], cn[:, SUBLANES - (CONV_W - 1):]), w_bf


def _lambda_value(lq1_ref, lk1_ref, lq2_ref, lk2_ref):
    s1 = jnp.sum(lq1_ref[...] * lk1_ref[...], axis=-1, keepdims=True)
    s2 = jnp.sum(lq2_ref[...] * lk2_ref[...], axis=-1, keepdims=True)
    return jnp.exp(s1) - jnp.exp(s2) + LAM_INIT


def _stack_components(q):
    lane = lax.broadcasted_iota(jnp.int32, q.shape, 1)
    zero = jnp.zeros_like(q)
    return jnp.concatenate([jnp.where(lane < QK_DIM, q, zero),
                            jnp.where(lane < QK_DIM, zero, q)], axis=0)


def _attn_prompt_body(lq1_ref, lk1_ref, lq2_ref, lk2_ref, g_ref, q_ref, k_ref, v_ref,
                      wo_ref, wu_ref, wd_ref, o_ref, wo_bf_ref, wu_bf_ref, wd_bf_ref,
                      kb_scr, vt_scr, s_scr, *, tq):
    for src, dst in ((wo_ref, wo_bf_ref), (wu_ref, wu_bf_ref), (wd_ref, wd_bf_ref)):
        dst[...] = src[...].astype(BF16)
    t = q_ref.shape[1]
    nq = t // tq
    kb_scr[...] = (k_ref[0] * LOG2E).astype(BF16)
    vt_scr[...] = v_ref[0].T.astype(BF16)
    lam = _lambda_value(lq1_ref, lk1_ref, lq2_ref, lk2_ref)
    gain = g_ref[...] * (1.0 - LAM_INIT)

    key_chunk = lax.broadcasted_iota(jnp.int32, (tq, 2 * tq), 0) // CHUNK
    qcol = lax.broadcasted_iota(jnp.int32, (tq, 2 * tq), 1)
    q_chunk = jnp.where(qcol >= tq, qcol - tq, qcol) // CHUNK
    visible = key_chunk <= q_chunk

    def fold(x):
        return x.reshape(tq // SUBLANES, SUBLANES, x.shape[-1])

    def stacked_q(i):
        return _stack_components(q_ref[0, i * tq:(i + 1) * tq, :])

    def score_chunk(i, c, qs, m8):
        sc = lax.dot_general(kb_scr[c * tq:(c + 1) * tq, :], qs, (((1,), (1,)), ((), ())),
                             preferred_element_type=F32)
        if c == i:
            sc = jnp.where(visible, sc, -1e30)
        s_scr[i % 2, c * tq:(c + 1) * tq, :] = sc
        return jnp.maximum(m8, jnp.max(fold(sc), axis=0))

    def value_chunk(i, c, m, l8, ot):
        p = jnp.exp2(s_scr[i % 2, c * tq:(c + 1) * tq, :] - m)
        ot = ot + jnp.dot(vt_scr[:, c * tq:(c + 1) * tq], p.astype(BF16),
                          preferred_element_type=F32)
        return l8 + jnp.sum(fold(p), axis=0), ot

    neg_inf = jnp.full((SUBLANES, 2 * tq), -jnp.inf, F32)
    qs = stacked_q(0)
    m8 = score_chunk(0, 0, qs, neg_inf)
    for i in range(nq):
        m = jnp.max(m8, axis=0, keepdims=True)
        l8 = jnp.zeros((SUBLANES, 2 * tq), F32)
        ot = jnp.zeros((V_DIM, 2 * tq), F32)
        if i + 1 < nq:
            qs, m8 = stacked_q(i + 1), neg_inf
        for c in range(i + 2):
            if c <= i:
                l8, ot = value_chunk(i, c, m, l8, ot)
            if i + 1 < nq:
                m8 = score_chunk(i + 1, c, qs, m8)
        ot = ot / jnp.sum(l8, axis=0, keepdims=True)
        o = ot[:, :tq] - lam * ot[:, tq:]
        ms = jnp.mean(o * o, axis=0, keepdims=True)
        o = o * lax.rsqrt(ms + EPS) * gain
        o_ref[0, i * tq:(i + 1) * tq, :] = o.T.astype(BF16)


def _attention_prompt(q, k, v, lams, g_subln, f32_weights, tq):
    b, t, aw = q.shape
    n_steps = b * N_HEADS
    head = pl.BlockSpec((1, t, V_DIM), lambda bi, h: (bi, 0, h))
    lam_spec = pl.BlockSpec((1, QK_DIM), lambda bi, h: (0, 0))

    def slice_spec(wt):
        rows = wt.shape[0] // n_steps
        assert rows * n_steps == wt.shape[0] and rows % (2 * SUBLANES) == 0
        return pl.BlockSpec((rows, wt.shape[1]), lambda bi, h: (bi * N_HEADS + h, 0))

    cast_specs = [slice_spec(wt) for wt in f32_weights]
    out, *casts = pl.pallas_call(
        functools.partial(_attn_prompt_body, tq=tq),
        grid=(b, N_HEADS),
        in_specs=[lam_spec] * 4 + [pl.BlockSpec((V_DIM, 1), lambda bi, h: (0, 0)),
                                   head, head, head] + cast_specs,
        out_specs=[head] + cast_specs,
        out_shape=[jax.ShapeDtypeStruct((b, t, aw), BF16)]
        + [jax.ShapeDtypeStruct(wt.shape, BF16) for wt in f32_weights],
        scratch_shapes=[pltpu.VMEM((t, V_DIM), BF16), pltpu.VMEM((V_DIM, t), BF16),
                        pltpu.VMEM((2, t, 2 * tq), F32)],
        compiler_params=_params(2),
        name="attention_prompt",
    )(*lams, g_subln.reshape(V_DIM, 1), q, k, v, *f32_weights)
    return out, casts


def _attn_sample_body(lq1_ref, lk1_ref, lq2_ref, lk2_ref, g_ref, q_ref, kt_ref, pv_ref,
                      nk_ref, nv_ref, carried_in_ref, o_ref, carried_out_ref, *, tq):
    lam = _lambda_value(lq1_ref, lk1_ref, lq2_ref, lk2_ref)
    gain = g_ref[...] * (1.0 - LAM_INIT)
    for h in range(N_HEADS):
        cols = slice(h * V_DIM, (h + 1) * V_DIM)
        qs = _stack_components(q_ref[0, :, cols])
        s_past = jnp.dot(qs, kt_ref[0, cols, :].astype(BF16), preferred_element_type=F32)
        s_new = lax.dot_general(qs, nk_ref[0, :, cols].astype(BF16), (((1,), (1,)), ((), ())),
                                preferred_element_type=F32)
        m = jnp.maximum(jnp.max(s_past, axis=-1, keepdims=True),
                        jnp.max(s_new, axis=-1, keepdims=True))
        p_past = jnp.exp(s_past - m)
        p_new = jnp.exp(s_new - m)
        l = jnp.sum(p_past, axis=-1, keepdims=True) + jnp.sum(p_new, axis=-1, keepdims=True)
        o = jnp.dot(p_past.astype(BF16), pv_ref[:, h, :].astype(BF16),
                    preferred_element_type=F32)
        o = o + jnp.dot(p_new.astype(BF16), nv_ref[0, :, cols].astype(BF16),
                        preferred_element_type=F32)
        o = o / l
        o = o[:tq] - lam * o[tq:]
        ms = jnp.mean(o * o, axis=-1, keepdims=True)
        o_ref[0, :, cols] = (o * lax.rsqrt(ms + EPS) * gain).astype(BF16)


def _attention_sample(q, past_kt, past_v, new_k, new_v, lams, g_subln, carried):
    b, t, aw = q.shape
    past = past_kt.shape[-1]
    assert past % CHUNK == 0 and t <= CHUNK
    row_new = pl.BlockSpec((1, t, aw), lambda bi: (bi, 0, 0))
    vec = lambda width: pl.BlockSpec((1, width), lambda bi: (0, 0))
    untouched = pl.BlockSpec(memory_space=pl.ANY)
    in_specs = [vec(QK_DIM)] * 4 + [
        vec(V_DIM), row_new,
        pl.BlockSpec((1, aw, past), lambda bi: (bi, 0, 0)),
        pl.BlockSpec((None, past, N_HEADS, V_DIM), lambda bi: (bi, 0, 0, 0)),
        row_new, row_new, untouched]
    return pl.pallas_call(
        functools.partial(_attn_sample_body, tq=t),
        grid=(b,),
        in_specs=in_specs,
        out_specs=[row_new, untouched],
        out_shape=[jax.ShapeDtypeStruct((b, t, aw), BF16),
                   jax.ShapeDtypeStruct(carried.shape, carried.dtype)],
        input_output_aliases={len(in_specs) - 1: 1},
        compiler_params=_params(1),
        name="attention_sample",
    )(*lams, g_subln, q, past_kt, past_v, new_k, new_v, carried)


def _mlp_body(x_ref, o_ref, y_ref, gt1_ref, sh2_ref, sc2_ref, gt2_ref, wo_ref, wu_ref, wd_ref,
              out_ref, xn_scr, acc_scr, *, bb, tt):
    f = pl.program_id(2)
    m = bb * tt
    d = x_ref.shape[-1]
    aw = o_ref.shape[-1]

    @pl.when(f == 0)
    def _():
        mix = jnp.dot(o_ref[...].reshape(m, aw), wo_ref[0:aw, :], preferred_element_type=F32)
        mix = mix + jnp.dot(y_ref[...].reshape(m, y_ref.shape[-1]), wo_ref[aw:, :],
                            preferred_element_type=F32)
        x1 = x_ref[...] + gt1_ref[...] * mix.reshape(bb, tt, d)
        out_ref[...] = x1
        ms = jnp.mean(x1 * x1, axis=-1, keepdims=True)
        xn = x1 * lax.rsqrt(ms + EPS) * (1.0 + sc2_ref[...]) + sh2_ref[...]
        xn_scr[...] = xn.reshape(m, d).astype(BF16)
        acc_scr[...] = jnp.zeros(acc_scr.shape, F32)

    hid = jnp.maximum(jnp.dot(xn_scr[...], wu_ref[...], preferred_element_type=F32), 0.0)
    acc_scr[...] += jnp.dot((hid * hid).astype(BF16), wd_ref[...], preferred_element_type=F32)

    @pl.when(f == pl.num_programs(2) - 1)
    def _():
        out_ref[...] = out_ref[...] + gt2_ref[...] * acc_scr[...].reshape(bb, tt, d)


def _outproj_mlp(x, o, y, mod, w_out, w_up, w_down, bb, tt, tf=MLP_HIDDEN_TILE):
    b, t, d = x.shape
    aw, lw = o.shape[-1], y.shape[-1]
    dff = w_up.shape[1]
    row = lambda width: pl.BlockSpec((bb, tt, width), lambda bi, i, f: (bi, i, 0))
    return pl.pallas_call(
        functools.partial(_mlp_body, bb=bb, tt=tt),
        grid=(b // bb, t // tt, dff // tf),
        in_specs=[row(d), row(aw), row(lw),
                  _mod_spec(bb, d, 2, 3), _mod_spec(bb, d, 3, 3),
                  _mod_spec(bb, d, 4, 3), _mod_spec(bb, d, 5, 3),
                  pl.BlockSpec((aw + lw, d), lambda bi, i, f: (0, 0),
                               pipeline_mode=pl.Buffered(1)),
                  pl.BlockSpec((d, tf), lambda bi, i, f: (0, f)),
                  pl.BlockSpec((tf, d), lambda bi, i, f: (f, 0))],
        out_specs=row(d),
        out_shape=jax.ShapeDtypeStruct((b, t, d), F32),
        scratch_shapes=[pltpu.VMEM((bb * tt, d), BF16), pltpu.VMEM((bb * tt, d), F32)],
        compiler_params=_params(3),
        name="outproj_mlp",
    )(x, o, y, mod, mod, mod, mod, w_out, w_up, w_down)


def kernel(x_prompt, x_sample, c_prompt, c_sample, cache_k, cache_v, state_lru_h, state_conv,
           w_ada, b_ada, w_in, g_q, g_k, lambda_q1, lambda_k1, lambda_q2, lambda_k2, g_subln,
           conv_w, conv_b, w_gate_a, b_gate_a, w_gate_x, b_gate_x, lru_lambda,
           w_out, w_up, w_down):
    bp, tp, d = x_prompt.shape
    bs, ts, _ = x_sample.shape
    past = cache_k.shape[2]
    aw = N_HEADS * V_DIM
    lw = conv_w.shape[-1]

    c_all = jnp.concatenate([c_prompt, c_sample], axis=0)
    mod = _modulation(c_all, w_ada[0], b_ada[0]).reshape(bp + bs, N_MOD, 1, d)
    mod_p, mod_s = mod[:bp], mod[bp:]

    rep = LANES // QK_DIM
    gq = jnp.tile(g_q[0], rep).reshape(1, LANES)
    gk = jnp.tile(g_k[0], rep).reshape(1, LANES)
    lru_weights = (conv_w[0], conv_b[0],
                   jnp.concatenate([w_gate_a[0], w_gate_x[0]], axis=-1).astype(BF16),
                   b_gate_a[0].reshape(-1), b_gate_x[0].reshape(-1), lru_lambda[0])
    lams = tuple(a[0].reshape(1, QK_DIM) for a in (lambda_q1, lambda_k1, lambda_q2, lambda_k2))
    g_sub = g_subln[0].reshape(1, V_DIM)

    (qs, ks, vs, ys, hs, cs), w_in_bf = _in_projection_streamed(
        x_sample, mod_s, w_in[0], gq, gk, _rope_tables(past + jnp.arange(ts)),
        state_lru_h[0], state_conv[0], lru_weights)
    past_kt = jnp.transpose(cache_k[0], (0, 2, 3, 4, 1)).reshape(bs, aw, past)
    osamp, w_in_bf = _attention_sample(qs, past_kt, cache_v[0], ks, vs, lams, g_sub, w_in_bf)

    tt_p = min(tp, ROW_TILE)
    qp, kp, vp, yp, hp, cp = _in_projection(
        x_prompt, mod_p, w_in_bf, gq, gk, _rope_tables(jnp.arange(tp)),
        jnp.zeros((bp, lw), F32), jnp.zeros((bp, CONV_W - 1, lw), F32), lru_weights, 1, tt_p)
    op, (wo, wu, wd) = _attention_prompt(qp, kp, vp, lams, g_sub,
                                         (w_out[0], w_up[0], w_down[0]), tq=min(tp, Q_BLOCK))
    out_p = _outproj_mlp(x_prompt, op, yp, mod_p, wo, wu, wd, 1, tt_p)
    out_s = _outproj_mlp(x_sample, osamp, ys, mod_s, wo, wu, wd, bs, ts)

    return (out_p, out_s,
            kp.reshape(1, bp, tp, N_HEADS, 2, QK_DIM), vp.reshape(1, bp, tp, N_HEADS, V_DIM),
            hp[None], cp[None],
            ks.reshape(1, bs, ts, N_HEADS, 2, QK_DIM), vs.reshape(1, bs, ts, N_HEADS, V_DIM),
            hs[None], cs[None])
```

```python
import functools
import math

import jax
import jax.numpy as jnp
from jax import lax
from jax.experimental import pallas as pl
from jax.experimental.pallas import tpu as pltpu

F32 = jnp.float32
BF16 = jnp.bfloat16

LANES = 128
SUBLANES = 8
VMEM_LIMIT = 56 * 1024 * 1024

ROW_TILE = 512
PROJ_COL_TILE = 512
MLP_HIDDEN_TILE = 1024
Q_BLOCK = 512
MOD_COL_TILE = 1024

N_HEADS = 8
QK_DIM = 64
V_DIM = 2 * QK_DIM
CHUNK = 64
ROT_DIM = QK_DIM // 4
ROPE_THETA = 500000.0
N_LRU_BLOCKS = 8
CONV_W = 4
LRU_C = 8.0
N_MOD = 6
EPS = 1e-6
LAM_INIT = 0.8 - 0.6 * math.exp(-0.3 * 0)
LOG2E = math.log2(math.e)


def _params(n_axes):
    return pltpu.CompilerParams(dimension_semantics=("arbitrary",) * n_axes,
                                vmem_limit_bytes=VMEM_LIMIT)


def _mod_body(c_ref, w_ref, b_ref, o_ref):
    c = c_ref[...]
    s = c * jax.nn.sigmoid(c)
    o_ref[...] = jnp.dot(s.astype(BF16), w_ref[...].astype(BF16),
                         preferred_element_type=F32) + b_ref[...]


def _modulation(c, w_ada, b_ada, tn=MOD_COL_TILE):
    rows, d = c.shape
    n = w_ada.shape[1]
    return pl.pallas_call(
        _mod_body,
        grid=(n // tn,),
        in_specs=[pl.BlockSpec((rows, d), lambda j: (0, 0)),
                  pl.BlockSpec((d, tn), lambda j: (0, j)),
                  pl.BlockSpec((1, tn), lambda j: (0, j))],
        out_specs=pl.BlockSpec((rows, tn), lambda j: (0, j)),
        out_shape=jax.ShapeDtypeStruct((rows, n), F32),
        compiler_params=_params(1),
        name="modulation",
    )(c, w_ada, b_ada.reshape(1, n))


def _mod_spec(bb, d, which, n_axes):
    if n_axes == 3:
        return pl.BlockSpec((bb, None, 1, d), lambda b, i, j: (b, which, 0, 0))
    return pl.BlockSpec((bb, None, 1, d), lambda b, i: (b, which, 0, 0))


def _rope_tables(pos):
    half = ROT_DIM // 2
    inv_freq = ROPE_THETA ** (-(jnp.arange(half, dtype=F32) * 2.0) / ROT_DIM)
    ang = pos.astype(F32)[:, None] * inv_freq[None, :]
    cos, sin = jnp.cos(ang), jnp.sin(ang)
    t = pos.shape[0]
    ones = jnp.ones((t, QK_DIM - ROT_DIM), F32)
    zeros_h = jnp.zeros((t, half), F32)
    zeros_r = jnp.zeros((t, QK_DIM - ROT_DIM), F32)
    c = jnp.concatenate([cos, cos, ones], axis=1)
    s1 = jnp.concatenate([zeros_h, sin, zeros_r], axis=1)
    s2 = jnp.concatenate([-sin, zeros_h, zeros_r], axis=1)
    rep = LANES // QK_DIM
    return tuple(jnp.tile(a, (1, rep)) for a in (c, s1, s2))


def _qk_norm_rope(acc, g, c, s1, s2):
    m, tn = acc.shape
    lane = lax.broadcasted_iota(jnp.int32, (m, LANES), 1)
    lo = lane < QK_DIM
    outs = []
    for t in range(tn // LANES):
        z = acc[:, t * LANES:(t + 1) * LANES]
        zz = z * z
        s_lo = jnp.sum(jnp.where(lo, zz, 0.0), axis=-1, keepdims=True)
        s_hi = jnp.sum(jnp.where(lo, 0.0, zz), axis=-1, keepdims=True)
        r = jnp.where(lo, lax.rsqrt(s_lo * (1.0 / QK_DIM) + EPS),
                      lax.rsqrt(s_hi * (1.0 / QK_DIM) + EPS))
        y = z * r * g
        y = (y * c + pltpu.roll(y, ROT_DIM // 2, 1) * s1
             + pltpu.roll(y, LANES - ROT_DIM // 2, 1) * s2)
        outs.append(y)
    return jnp.concatenate(outs, axis=1)


def _gelu_tanh(x):
    return 0.5 * x * (1.0 + jnp.tanh(math.sqrt(2.0 / math.pi) * (x + 0.044715 * (x * x * x))))


def _lru_tile(ext, gb, h_prev, cw, cb, wax_ref, ba, bx, lam):
    tt = ext.shape[0] - SUBLANES
    w = ext.shape[1]
    n_blocks = wax_ref.shape[0]
    blk = w // n_blocks
    groups = tt // SUBLANES
    first = SUBLANES - (CONV_W - 1)
    u = cb + cw[0:1, :] * ext[pl.ds(first, tt), :]
    for jj in range(1, CONV_W):
        u = u + cw[jj:jj + 1, :] * ext[pl.ds(first + jj, tt), :]

    ub = u.astype(BF16)
    ga, gx = [], []
    for n in range(n_blocks):
        res = jnp.dot(ub[:, n * blk:(n + 1) * blk], wax_ref[n], preferred_element_type=F32)
        ga.append(res[:, :blk])
        gx.append(res[:, blk:])
    r = jax.nn.sigmoid(jnp.concatenate(ga, axis=1) + ba)
    ig = jax.nn.sigmoid(jnp.concatenate(gx, axis=1) + bx)

    nl = -lam
    softplus = jnp.maximum(nl, 0.0) + jnp.log1p(jnp.exp(-jnp.abs(nl)))
    log_a = -LRU_C * r * softplus
    a = jnp.exp(log_a)
    b = jnp.sqrt(-jnp.tanh(log_a) * (a * a + 1.0)) * (ig * u)

    a3 = a.reshape(groups, SUBLANES, w)
    b3 = b.reshape(groups, SUBLANES, w)
    row = lax.broadcasted_iota(jnp.int32, (groups, SUBLANES, w), 1)
    for s in (1, 2, 4):
        keep = row >= s
        b3 = jnp.where(keep, a3 * pltpu.roll(b3, s, 1) + b3, b3)
        a3 = jnp.where(keep, a3 * pltpu.roll(a3, s, 1), a3)
    hs = []
    for g in range(groups):
        hg = a3[g] * h_prev + b3[g]
        hs.append(hg)
        h_prev = hg[SUBLANES - 1:SUBLANES, :]
    h = jnp.concatenate(hs, axis=0)
    return h * _gelu_tanh(gb), h_prev


def _inproj_body(x_ref, sh_ref, sc_ref, w_ref, gq_ref, gk_ref, rc_ref, rs1_ref, rs2_ref,
                 h0_ref, c0_ref, cw_ref, cb_ref, wax_ref, ba_ref, bx_ref, lam_ref,
                 q_ref, k_ref, v_ref, y_ref, hn_ref, cn_ref,
                 xn_scr, ext_scr, gb_scr, h_scr, *, bb, tt, tn):
    i = pl.program_id(1)
    m = bb * tt
    d = x_ref.shape[-1]
    x = x_ref[...]
    ms = jnp.mean(x * x, axis=-1, keepdims=True)
    xn = x * lax.rsqrt(ms + EPS) * (1.0 + sc_ref[...]) + sh_ref[...]
    xn_scr[...] = xn.reshape(m, d).astype(BF16)

    @pl.when(i == 0)
    def _():
        ext_scr[:, 0:SUBLANES, :] = c0_ref[...]
        h_scr[...] = h0_ref[...]

    sec_w = q_ref.shape[-1]
    tiles = sec_w // tn

    def project(s, h):
        col = s * sec_w + h * tn
        return jnp.dot(xn_scr[...], w_ref[:, col:col + tn], preferred_element_type=F32)

    for h in range(tiles):
        ext_scr[:, SUBLANES:, h * tn:(h + 1) * tn] = project(3, h).reshape(bb, tt, tn)
    for h in range(tiles):
        gb_scr[:, h * tn:(h + 1) * tn] = project(4, h)
    for b in range(bb):
        y, h_last = _lru_tile(ext_scr.at[b], gb_scr[b * tt:(b + 1) * tt, :], h_scr[b],
                              cw_ref[...], cb_ref[...], wax_ref, ba_ref[...], bx_ref[...],
                              lam_ref[...])
        y_ref[b] = y.astype(BF16)
        h_scr[b] = h_last
    ext_scr[:, 0:SUBLANES, :] = ext_scr[:, tt:tt + SUBLANES, :]
    hn_ref[...] = h_scr[...]
    cn_ref[...] = ext_scr[:, 0:SUBLANES, :]

    tabs = (rc_ref[...], rs1_ref[...], rs2_ref[...])
    if bb > 1:
        tabs = tuple(jnp.concatenate([a] * bb, axis=0) for a in tabs)
    for h in range(tiles):
        acc = _qk_norm_rope(project(0, h), gq_ref[...], *tabs) * (QK_DIM ** -0.5)
        q_ref[:, :, h * tn:(h + 1) * tn] = acc.astype(BF16).reshape(bb, tt, tn)
    for h in range(tiles):
        acc = _qk_norm_rope(project(1, h), gk_ref[...], *tabs)
        k_ref[:, :, h * tn:(h + 1) * tn] = acc.reshape(bb, tt, tn)
    for h in range(tiles):
        v_ref[:, :, h * tn:(h + 1) * tn] = project(2, h).reshape(bb, tt, tn)


def _in_projection(x, mod, w_in, g_q, g_k, ropes, h0, conv0, lru_weights, bb, tt,
                   tn=PROJ_COL_TILE):
    b, t, d = x.shape
    n = w_in.shape[1]
    sec_w = n // 5
    rc, rs1, rs2 = ropes
    conv_w, conv_b, w_ax, b_a, b_x, lru_lambda = lru_weights
    w = conv_w.shape[-1]
    blk = w // N_LRU_BLOCKS
    c0 = jnp.pad(conv0, ((0, 0), (SUBLANES - (CONV_W - 1), 0), (0, 0)))
    sec_spec = pl.BlockSpec((bb, tt, sec_w), lambda bi, i: (bi, i, 0))
    tab_spec = pl.BlockSpec((tt, LANES), lambda bi, i: (i, 0))
    g_spec = pl.BlockSpec((1, LANES), lambda bi, i: (0, 0))
    state_spec = lambda rows: pl.BlockSpec((bb, rows, w), lambda bi, i: (bi, 0, 0))
    vec_spec = lambda rows: pl.BlockSpec((rows, w), lambda bi, i: (0, 0))
    f32_out = jax.ShapeDtypeStruct((b, t, sec_w), F32)
    q, k, v, y, hn, cn = pl.pallas_call(
        functools.partial(_inproj_body, bb=bb, tt=tt, tn=tn),
        grid=(b // bb, t // tt),
        in_specs=[pl.BlockSpec((bb, tt, d), lambda bi, i: (bi, i, 0)),
                  _mod_spec(bb, d, 0, 2), _mod_spec(bb, d, 1, 2),
                  pl.BlockSpec((d, n), lambda bi, i: (0, 0), pipeline_mode=pl.Buffered(1)),
                  g_spec, g_spec, tab_spec, tab_spec, tab_spec,
                  state_spec(1), state_spec(SUBLANES), vec_spec(CONV_W), vec_spec(1),
                  pl.BlockSpec((N_LRU_BLOCKS, blk, 2 * blk), lambda bi, i: (0, 0, 0)),
                  vec_spec(1), vec_spec(1), vec_spec(1)],
        out_specs=[sec_spec, sec_spec, sec_spec, sec_spec, state_spec(1), state_spec(SUBLANES)],
        out_shape=[jax.ShapeDtypeStruct((b, t, sec_w), BF16), f32_out, f32_out,
                   jax.ShapeDtypeStruct((b, t, w), BF16),
                   jax.ShapeDtypeStruct((b, 1, w), F32),
                   jax.ShapeDtypeStruct((b, SUBLANES, w), F32)],
        scratch_shapes=[pltpu.VMEM((bb * tt, d), BF16),
                        pltpu.VMEM((bb, tt + SUBLANES, w), F32),
                        pltpu.VMEM((bb * tt, w), F32),
                        pltpu.VMEM((bb, 1, w), F32)],
        compiler_params=_params(2),
        name="in_projection",
    )(x, mod, mod, w_in, g_q, g_k, rc, rs1, rs2, h0.reshape(b, 1, w), c0,
      conv_w, conv_b.reshape(1, w), w_ax, b_a.reshape(1, w), b_x.reshape(1, w),
      lru_lambda.reshape(1, w))
    return q, k, v, y, hn[:, 0], cn[:, SUBLANES - (CONV_W - 1):]


def _inproj_stream_body(x_ref, sh_ref, sc_ref, w_ref, gq_ref, gk_ref, rc_ref, rs1_ref, rs2_ref,
                        h0_ref, c0_ref, cw_ref, cb_ref, wax_ref, ba_ref, bx_ref, lam_ref,
                        q_ref, k_ref, v_ref, y_ref, hn_ref, cn_ref, wbf_ref,
                        xn_scr, ext_scr, gb_scr, *, bb, tt, tn):
    j = pl.program_id(0)
    m = bb * tt
    d = x_ref.shape[-1]

    @pl.when(j == 0)
    def _():
        x = x_ref[...]
        ms = jnp.mean(x * x, axis=-1, keepdims=True)
        xn = x * lax.rsqrt(ms + EPS) * (1.0 + sc_ref[...]) + sh_ref[...]
        xn_scr[...] = xn.reshape(m, d).astype(BF16)
        ext_scr[:, 0:SUBLANES, :] = c0_ref[...]

    wt = w_ref[...].astype(BF16)
    wbf_ref[...] = wt
    acc = jnp.dot(xn_scr[...], wt, preferred_element_type=F32)

    sec_w = q_ref.shape[-1]
    tiles = sec_w // tn

    def tables():
        tabs = (rc_ref[...], rs1_ref[...], rs2_ref[...])
        return tuple(jnp.concatenate([a] * bb, axis=0) for a in tabs) if bb > 1 else tabs

    for c in range(5 * tiles):
        s, h = divmod(c, tiles)
        cols = slice(h * tn, (h + 1) * tn)

        @pl.when(j == c)
        def _(s=s, h=h, cols=cols):
            if s == 0:
                y = _qk_norm_rope(acc, gq_ref[...], *tables()) * (QK_DIM ** -0.5)
                q_ref[:, :, cols] = y.astype(BF16).reshape(bb, tt, tn)
            elif s == 1:
                k_ref[:, :, cols] = _qk_norm_rope(acc, gk_ref[...], *tables()).reshape(bb, tt, tn)
            elif s == 2:
                v_ref[:, :, cols] = acc.reshape(bb, tt, tn)
            elif s == 3:
                ext_scr[:, SUBLANES:, cols] = acc.reshape(bb, tt, tn)
            else:
                gb_scr[:, cols] = acc
                if h == tiles - 1:
                    for b in range(bb):
                        y, h_last = _lru_tile(ext_scr.at[b], gb_scr[b * tt:(b + 1) * tt, :],
                                              h0_ref[b], cw_ref[...], cb_ref[...], wax_ref,
                                              ba_ref[...], bx_ref[...], lam_ref[...])
                        y_ref[b] = y.astype(BF16)
                        hn_ref[b] = h_last
                    cn_ref[...] = ext_scr[:, tt:tt + SUBLANES, :]


def _in_projection_streamed(x, mod, w_in_f32, g_q, g_k, ropes, h0, conv0, lru_weights,
                            tn=PROJ_COL_TILE):
    b, t, d = x.shape
    n = w_in_f32.shape[1]
    sec_w = n // 5
    rc, rs1, rs2 = ropes
    conv_w, conv_b, w_ax, b_a, b_x, lru_lambda = lru_weights
    w = conv_w.shape[-1]
    blk = w // N_LRU_BLOCKS
    c0 = jnp.pad(conv0, ((0, 0), (SUBLANES - (CONV_W - 1), 0), (0, 0)))
    whole = lambda *shape: pl.BlockSpec(shape, lambda j: (0,) * len(shape))
    mod_spec = lambda which: pl.BlockSpec((b, None, 1, d), lambda j: (0, which, 0, 0))
    col_tile = pl.BlockSpec((d, tn), lambda j: (0, j))
    f32_out = jax.ShapeDtypeStruct((b, t, sec_w), F32)
    q, k, v, y, hn, cn, w_bf = pl.pallas_call(
        functools.partial(_inproj_stream_body, bb=b, tt=t, tn=tn),
        grid=(n // tn,),
        in_specs=[whole(b, t, d), mod_spec(0), mod_spec(1), col_tile,
                  whole(1, LANES), whole(1, LANES),
                  whole(t, LANES), whole(t, LANES), whole(t, LANES),
                  whole(b, 1, w), whole(b, SUBLANES, w), whole(CONV_W, w), whole(1, w),
                  whole(N_LRU_BLOCKS, blk, 2 * blk), whole(1, w), whole(1, w), whole(1, w)],
        out_specs=[whole(b, t, sec_w), whole(b, t, sec_w), whole(b, t, sec_w), whole(b, t, w),
                   whole(b, 1, w), whole(b, SUBLANES, w), col_tile],
        out_shape=[jax.ShapeDtypeStruct((b, t, sec_w), BF16), f32_out, f32_out,
                   jax.ShapeDtypeStruct((b, t, w), BF16),
                   jax.ShapeDtypeStruct((b, 1, w), F32),
                   jax.ShapeDtypeStruct((b, SUBLANES, w), F32),
                   jax.ShapeDtypeStruct((d, n), BF16)],
        scratch_shapes=[pltpu.VMEM((b * t, d), BF16),
                        pltpu.VMEM((b, t + SUBLANES, w), F32),
                        pltpu.VMEM((b * t, w), F32)],
        compiler_params=_params(1),
        name="in_projection_streamed",
    )(x, mod, mod, w_in_f32, g_q, g_k, rc, rs1, rs2, h0.reshape(b, 1, w), c0,
      conv_w, conv_b.reshape(1, w), w_ax, b_a.reshape(1, w), b_x.reshape(1, w),
      lru_lambda.reshape(1, w))
    return (q, k, v, y, hn[:, 0], cn[:, SUBLANES - (CONV_W - 1):]), w_bf


def _lambda_value(lq1_ref, lk1_ref, lq2_ref, lk2_ref):
    s1 = jnp.sum(lq1_ref[...] * lk1_ref[...], axis=-1, keepdims=True)
    s2 = jnp.sum(lq2_ref[...] * lk2_ref[...], axis=-1, keepdims=True)
    return jnp.exp(s1) - jnp.exp(s2) + LAM_INIT


def _stack_components(q):
    lane = lax.broadcasted_iota(jnp.int32, q.shape, 1)
    zero = jnp.zeros_like(q)
    return jnp.concatenate([jnp.where(lane < QK_DIM, q, zero),
                            jnp.where(lane < QK_DIM, zero, q)], axis=0)


def _attn_prompt_body(lq1_ref, lk1_ref, lq2_ref, lk2_ref, g_ref, q_ref, k_ref, v_ref,
                      wo_ref, wu_ref, wd_ref, o_ref, wo_bf_ref, wu_bf_ref, wd_bf_ref,
                      kb_scr, vt_scr, s_scr, *, tq):
    for src, dst in ((wo_ref, wo_bf_ref), (wu_ref, wu_bf_ref), (wd_ref, wd_bf_ref)):
        dst[...] = src[...].astype(BF16)
    t = q_ref.shape[1]
    nq = t // tq
    kb_scr[...] = (k_ref[0] * LOG2E).astype(BF16)
    vt_scr[...] = v_ref[0].T.astype(BF16)
    lam = _lambda_value(lq1_ref, lk1_ref, lq2_ref, lk2_ref)
    gain = g_ref[...] * (1.0 - LAM_INIT)

    key_chunk = lax.broadcasted_iota(jnp.int32, (tq, 2 * tq), 0) // CHUNK
    qcol = lax.broadcasted_iota(jnp.int32, (tq, 2 * tq), 1)
    q_chunk = jnp.where(qcol >= tq, qcol - tq, qcol) // CHUNK
    visible = key_chunk <= q_chunk

    def fold(x):
        return x.reshape(tq // SUBLANES, SUBLANES, x.shape[-1])

    def stacked_q(i):
        return _stack_components(q_ref[0, i * tq:(i + 1) * tq, :])

    def score_chunk(i, c, qs, m8):
        sc = lax.dot_general(kb_scr[c * tq:(c + 1) * tq, :], qs, (((1,), (1,)), ((), ())),
                             preferred_element_type=F32)
        if c == i:
            sc = jnp.where(visible, sc, -1e30)
        s_scr[i % 2, c * tq:(c + 1) * tq, :] = sc
        return jnp.maximum(m8, jnp.max(fold(sc), axis=0))

    def value_chunk(i, c, m, l8, ot):
        p = jnp.exp2(s_scr[i % 2, c * tq:(c + 1) * tq, :] - m)
        ot = ot + jnp.dot(vt_scr[:, c * tq:(c + 1) * tq], p.astype(BF16),
                          preferred_element_type=F32)
        return l8 + jnp.sum(fold(p), axis=0), ot

    neg_inf = jnp.full((SUBLANES, 2 * tq), -jnp.inf, F32)
    qs = stacked_q(0)
    m8 = score_chunk(0, 0, qs, neg_inf)
    for i in range(nq):
        m = jnp.max(m8, axis=0, keepdims=True)
        l8 = jnp.zeros((SUBLANES, 2 * tq), F32)
        ot = jnp.zeros((V_DIM, 2 * tq), F32)
        if i + 1 < nq:
            qs, m8 = stacked_q(i + 1), neg_inf
        for c in range(i + 2):
            if c <= i:
                l8, ot = value_chunk(i, c, m, l8, ot)
            if i + 1 < nq:
                m8 = score_chunk(i + 1, c, qs, m8)
        ot = ot / jnp.sum(l8, axis=0, keepdims=True)
        o = ot[:, :tq] - lam * ot[:, tq:]
        ms = jnp.mean(o * o, axis=0, keepdims=True)
        o = o * lax.rsqrt(ms + EPS) * gain
        o_ref[0, i * tq:(i + 1) * tq, :] = o.T.astype(BF16)


def _attention_prompt(q, k, v, lams, g_subln, f32_weights, tq):
    b, t, aw = q.shape
    n_steps = b * N_HEADS
    head = pl.BlockSpec((1, t, V_DIM), lambda bi, h: (bi, 0, h))
    lam_spec = pl.BlockSpec((1, QK_DIM), lambda bi, h: (0, 0))

    def slice_spec(wt):
        rows = wt.shape[0] // n_steps
        assert rows * n_steps == wt.shape[0] and rows % (2 * SUBLANES) == 0
        return pl.BlockSpec((rows, wt.shape[1]), lambda bi, h: (bi * N_HEADS + h, 0))

    cast_specs = [slice_spec(wt) for wt in f32_weights]
    out, *casts = pl.pallas_call(
        functools.partial(_attn_prompt_body, tq=tq),
        grid=(b, N_HEADS),
        in_specs=[lam_spec] * 4 + [pl.BlockSpec((V_DIM, 1), lambda bi, h: (0, 0)),
                                   head, head, head] + cast_specs,
        out_specs=[head] + cast_specs,
        out_shape=[jax.ShapeDtypeStruct((b, t, aw), BF16)]
        + [jax.ShapeDtypeStruct(wt.shape, BF16) for wt in f32_weights],
        scratch_shapes=[pltpu.VMEM((t, V_DIM), BF16), pltpu.VMEM((V_DIM, t), BF16),
                        pltpu.VMEM((2, t, 2 * tq), F32)],
        compiler_params=_params(2),
        name="attention_prompt",
    )(*lams, g_subln.reshape(V_DIM, 1), q, k, v, *f32_weights)
    return out, casts


def _attn_sample_body(lq1_ref, lk1_ref, lq2_ref, lk2_ref, g_ref, q_ref, kt_ref, pv_ref,
                      nk_ref, nv_ref, carried_in_ref, o_ref, carried_out_ref, *, tq):
    lam = _lambda_value(lq1_ref, lk1_ref, lq2_ref, lk2_ref)
    gain = g_ref[...] * (1.0 - LAM_INIT)
    for h in range(N_HEADS):
        cols = slice(h * V_DIM, (h + 1) * V_DIM)
        qs = _stack_components(q_ref[0, :, cols])
        s_past = jnp.dot(qs, kt_ref[0, cols, :].astype(BF16), preferred_element_type=F32)
        s_new = lax.dot_general(qs, nk_ref[0, :, cols].astype(BF16), (((1,), (1,)), ((), ())),
                                preferred_element_type=F32)
        m = jnp.maximum(jnp.max(s_past, axis=-1, keepdims=True),
                        jnp.max(s_new, axis=-1, keepdims=True))
        p_past = jnp.exp(s_past - m)
        p_new = jnp.exp(s_new - m)
        l = jnp.sum(p_past, axis=-1, keepdims=True) + jnp.sum(p_new, axis=-1, keepdims=True)
        o = jnp.dot(p_past.astype(BF16), pv_ref[:, h, :].astype(BF16),
                    preferred_element_type=F32)
        o = o + jnp.dot(p_new.astype(BF16), nv_ref[0, :, cols].astype(BF16),
                        preferred_element_type=F32)
        o = o / l
        o = o[:tq] - lam * o[tq:]
        ms = jnp.mean(o * o, axis=-1, keepdims=True)
        o_ref[0, :, cols] = (o * lax.rsqrt(ms + EPS) * gain).astype(BF16)


def _attention_sample(q, past_kt, past_v, new_k, new_v, lams, g_subln, carried):
    b, t, aw = q.shape
    past = past_kt.shape[-1]
    assert past % CHUNK == 0 and t <= CHUNK
    row_new = pl.BlockSpec((1, t, aw), lambda bi: (bi, 0, 0))
    vec = lambda width: pl.BlockSpec((1, width), lambda bi: (0, 0))
    untouched = pl.BlockSpec(memory_space=pl.ANY)
    in_specs = [vec(QK_DIM)] * 4 + [
        vec(V_DIM), row_new,
        pl.BlockSpec((1, aw, past), lambda bi: (bi, 0, 0)),
        pl.BlockSpec((None, past, N_HEADS, V_DIM), lambda bi: (bi, 0, 0, 0)),
        row_new, row_new, untouched]
    return pl.pallas_call(
        functools.partial(_attn_sample_body, tq=t),
        grid=(b,),
        in_specs=in_specs,
        out_specs=[row_new, untouched],
        out_shape=[jax.ShapeDtypeStruct((b, t, aw), BF16),
                   jax.ShapeDtypeStruct(carried.shape, carried.dtype)],
        input_output_aliases={len(in_specs) - 1: 1},
        compiler_params=_params(1),
        name="attention_sample",
    )(*lams, g_subln, q, past_kt, past_v, new_k, new_v, carried)


def _mlp_body(x_ref, o_ref, y_ref, gt1_ref, sh2_ref, sc2_ref, gt2_ref, wo_ref, wu_ref, wd_ref,
              out_ref, xn_scr, hid0_scr, hid1_scr, acc_scr, *, bb, tt, nf):
    f = pl.program_id(2)
    hid_scr = (hid0_scr, hid1_scr)
    m = bb * tt
    d = x_ref.shape[-1]
    aw = o_ref.shape[-1]

    @pl.when(f == 0)
    def _():
        mix = jnp.dot(o_ref[...].reshape(m, aw), wo_ref[0:aw, :], preferred_element_type=F32)
        mix = mix + jnp.dot(y_ref[...].reshape(m, y_ref.shape[-1]), wo_ref[aw:, :],
                            preferred_element_type=F32)
        x1 = x_ref[...] + gt1_ref[...] * mix.reshape(bb, tt, d)
        out_ref[...] = x1
        ms = jnp.mean(x1 * x1, axis=-1, keepdims=True)
        xn = x1 * lax.rsqrt(ms + EPS) * (1.0 + sc2_ref[...]) + sh2_ref[...]
        xn_scr[...] = xn.reshape(m, d).astype(BF16)
        acc_scr[...] = jnp.zeros(acc_scr.shape, F32)

    def up(slot):
        hid = jnp.maximum(jnp.dot(xn_scr[...], wu_ref[...], preferred_element_type=F32), 0.0)
        hid_scr[slot][...] = (hid * hid).astype(BF16)

    def down(slot):
        acc_scr[...] += jnp.dot(hid_scr[slot][...], wd_ref[...], preferred_element_type=F32)

    @pl.when(f == 0)
    def _():
        up(0)

    middle = jnp.logical_and(f > 0, f < nf)
    for parity in (0, 1):
        @pl.when(jnp.logical_and(middle, f % 2 == parity))
        def _(parity=parity):
            down(1 - parity)
            up(parity)

    @pl.when(f == nf)
    def _():
        down((nf - 1) % 2)
        out_ref[...] = out_ref[...] + gt2_ref[...] * acc_scr[...].reshape(bb, tt, d)


def _outproj_mlp(x, o, y, mod, w_out, w_up, w_down, bb, tt, tf=MLP_HIDDEN_TILE):
    b, t, d = x.shape
    aw, lw = o.shape[-1], y.shape[-1]
    nf = w_up.shape[1] // tf
    row = lambda width: pl.BlockSpec((bb, tt, width), lambda bi, i, f: (bi, i, 0))
    return pl.pallas_call(
        functools.partial(_mlp_body, bb=bb, tt=tt, nf=nf),
        grid=(b // bb, t // tt, nf + 1),
        in_specs=[row(d), row(aw), row(lw),
                  _mod_spec(bb, d, 2, 3), _mod_spec(bb, d, 3, 3),
                  _mod_spec(bb, d, 4, 3), _mod_spec(bb, d, 5, 3),
                  pl.BlockSpec((aw + lw, d), lambda bi, i, f: (0, 0),
                               pipeline_mode=pl.Buffered(1)),
                  pl.BlockSpec((d, tf), lambda bi, i, f: (0, jnp.minimum(f, nf - 1))),
                  pl.BlockSpec((tf, d), lambda bi, i, f: (jnp.maximum(f - 1, 0), 0))],
        out_specs=row(d),
        out_shape=jax.ShapeDtypeStruct((b, t, d), F32),
        scratch_shapes=[pltpu.VMEM((bb * tt, d), BF16), pltpu.VMEM((bb * tt, tf), BF16),
                        pltpu.VMEM((bb * tt, tf), BF16), pltpu.VMEM((bb * tt, d), F32)],
        compiler_params=_params(3),
        name="outproj_mlp",
    )(x, o, y, mod, mod, mod, mod, w_out, w_up, w_down)


def kernel(x_prompt, x_sample, c_prompt, c_sample, cache_k, cache_v, state_lru_h, state_conv,
           w_ada, b_ada, w_in, g_q, g_k, lambda_q1, lambda_k1, lambda_q2, lambda_k2, g_subln,
           conv_w, conv_b, w_gate_a, b_gate_a, w_gate_x, b_gate_x, lru_lambda,
           w_out, w_up, w_down):
    bp, tp, d = x_prompt.shape
    bs, ts, _ = x_sample.shape
    past = cache_k.shape[2]
    aw = N_HEADS * V_DIM
    lw = conv_w.shape[-1]

    c_all = jnp.concatenate([c_prompt, c_sample], axis=0)
    mod = _modulation(c_all, w_ada[0], b_ada[0]).reshape(bp + bs, N_MOD, 1, d)
    mod_p, mod_s = mod[:bp], mod[bp:]

    rep = LANES // QK_DIM
    gq = jnp.tile(g_q[0], rep).reshape(1, LANES)
    gk = jnp.tile(g_k[0], rep).reshape(1, LANES)
    lru_weights = (conv_w[0], conv_b[0],
                   jnp.concatenate([w_gate_a[0], w_gate_x[0]], axis=-1).astype(BF16),
                   b_gate_a[0].reshape(-1), b_gate_x[0].reshape(-1), lru_lambda[0])
    lams = tuple(a[0].reshape(1, QK_DIM) for a in (lambda_q1, lambda_k1, lambda_q2, lambda_k2))
    g_sub = g_subln[0].reshape(1, V_DIM)

    (qs, ks, vs, ys, hs, cs), w_in_bf = _in_projection_streamed(
        x_sample, mod_s, w_in[0], gq, gk, _rope_tables(past + jnp.arange(ts)),
        state_lru_h[0], state_conv[0], lru_weights)
    past_kt = jnp.transpose(cache_k[0], (0, 2, 3, 4, 1)).reshape(bs, aw, past)
    osamp, w_in_bf = _attention_sample(qs, past_kt, cache_v[0], ks, vs, lams, g_sub, w_in_bf)

    tt_p = min(tp, ROW_TILE)
    qp, kp, vp, yp, hp, cp = _in_projection(
        x_prompt, mod_p, w_in_bf, gq, gk, _rope_tables(jnp.arange(tp)),
        jnp.zeros((bp, lw), F32), jnp.zeros((bp, CONV_W - 1, lw), F32), lru_weights, 1, tt_p)
    op, (wo, wu, wd) = _attention_prompt(qp, kp, vp, lams, g_sub,
                                         (w_out[0], w_up[0], w_down[0]), tq=min(tp, Q_BLOCK))
    out_p = _outproj_mlp(x_prompt, op, yp, mod_p, wo, wu, wd, 1, tt_p)
    out_s = _outproj_mlp(x_sample, osamp, ys, mod_s, wo, wu, wd, bs, ts)

    return (out_p, out_s,
            kp.reshape(1, bp, tp, N_HEADS, 2, QK_DIM), vp.reshape(1, bp, tp, N_HEADS, V_DIM),
            hp[None], cp[None],
            ks.reshape(1, bs, ts, N_HEADS, 2, QK_DIM), vs.reshape(1, bs, ts, N_HEADS, V_DIM),
            hs[None], cs[None])
```

```python
import functools
import math

import jax
import jax.numpy as jnp
from jax import lax
from jax.experimental import pallas as pl
from jax.experimental.pallas import tpu as pltpu

F32 = jnp.float32
BF16 = jnp.bfloat16

LANES = 128
SUBLANES = 8
VMEM_LIMIT = 56 * 1024 * 1024

ROW_TILE = 512
PROJ_COL_TILE = 512
MLP_HIDDEN_TILE = 1024
Q_BLOCK = 512
MOD_COL_TILE = 1024

N_HEADS = 8
QK_DIM = 64
V_DIM = 2 * QK_DIM
CHUNK = 64
ROT_DIM = QK_DIM // 4
ROPE_THETA = 500000.0
N_LRU_BLOCKS = 8
CONV_W = 4
LRU_C = 8.0
N_MOD = 6
EPS = 1e-6
LAM_INIT = 0.8 - 0.6 * math.exp(-0.3 * 0)
LOG2E = math.log2(math.e)


def _params(n_axes):
    return pltpu.CompilerParams(dimension_semantics=("arbitrary",) * n_axes,
                                vmem_limit_bytes=VMEM_LIMIT)


def _mod_body(c_ref, w_ref, b_ref, o_ref):
    c = c_ref[...]
    s = c * jax.nn.sigmoid(c)
    o_ref[...] = jnp.dot(s.astype(BF16), w_ref[...].astype(BF16),
                         preferred_element_type=F32) + b_ref[...]


def _modulation(c, w_ada, b_ada, tn=MOD_COL_TILE):
    rows, d = c.shape
    n = w_ada.shape[1]
    return pl.pallas_call(
        _mod_body,
        grid=(n // tn,),
        in_specs=[pl.BlockSpec((rows, d), lambda j: (0, 0)),
                  pl.BlockSpec((d, tn), lambda j: (0, j)),
                  pl.BlockSpec((1, tn), lambda j: (0, j))],
        out_specs=pl.BlockSpec((rows, tn), lambda j: (0, j)),
        out_shape=jax.ShapeDtypeStruct((rows, n), F32),
        compiler_params=_params(1),
        name="modulation",
    )(c, w_ada, b_ada.reshape(1, n))


def _mod_spec(bb, d, which, n_axes):
    if n_axes == 3:
        return pl.BlockSpec((bb, None, 1, d), lambda b, i, j: (b, which, 0, 0))
    return pl.BlockSpec((bb, None, 1, d), lambda b, i: (b, which, 0, 0))


def _rope_tables(pos):
    half = ROT_DIM // 2
    inv_freq = ROPE_THETA ** (-(jnp.arange(half, dtype=F32) * 2.0) / ROT_DIM)
    ang = pos.astype(F32)[:, None] * inv_freq[None, :]
    cos, sin = jnp.cos(ang), jnp.sin(ang)
    t = pos.shape[0]
    ones = jnp.ones((t, QK_DIM - ROT_DIM), F32)
    zeros_h = jnp.zeros((t, half), F32)
    zeros_r = jnp.zeros((t, QK_DIM - ROT_DIM), F32)
    c = jnp.concatenate([cos, cos, ones], axis=1)
    s1 = jnp.concatenate([zeros_h, sin, zeros_r], axis=1)
    s2 = jnp.concatenate([-sin, zeros_h, zeros_r], axis=1)
    rep = LANES // QK_DIM
    return tuple(jnp.tile(a, (1, rep)) for a in (c, s1, s2))


def _qk_norm_rope(acc, g, c, s1, s2):
    m, tn = acc.shape
    lane = lax.broadcasted_iota(jnp.int32, (m, LANES), 1)
    lo = lane < QK_DIM
    outs = []
    for t in range(tn // LANES):
        z = acc[:, t * LANES:(t + 1) * LANES]
        zz = z * z
        s_lo = jnp.sum(jnp.where(lo, zz, 0.0), axis=-1, keepdims=True)
        s_hi = jnp.sum(jnp.where(lo, 0.0, zz), axis=-1, keepdims=True)
        r = jnp.where(lo, lax.rsqrt(s_lo * (1.0 / QK_DIM) + EPS),
                      lax.rsqrt(s_hi * (1.0 / QK_DIM) + EPS))
        y = z * r * g
        y = (y * c + pltpu.roll(y, ROT_DIM // 2, 1) * s1
             + pltpu.roll(y, LANES - ROT_DIM // 2, 1) * s2)
        outs.append(y)
    return jnp.concatenate(outs, axis=1)


def _gelu_tanh(x):
    return 0.5 * x * (1.0 + jnp.tanh(math.sqrt(2.0 / math.pi) * (x + 0.044715 * (x * x * x))))


def _lru_tile(ext, gb, h_prev, cw, cb, wax_ref, ba, bx, lam):
    tt = ext.shape[0] - SUBLANES
    w = ext.shape[1]
    n_blocks = wax_ref.shape[0]
    blk = w // n_blocks
    groups = tt // SUBLANES
    first = SUBLANES - (CONV_W - 1)
    u = cb + cw[0:1, :] * ext[pl.ds(first, tt), :]
    for jj in range(1, CONV_W):
        u = u + cw[jj:jj + 1, :] * ext[pl.ds(first + jj, tt), :]

    ub = u.astype(BF16)
    ga, gx = [], []
    for n in range(n_blocks):
        res = jnp.dot(ub[:, n * blk:(n + 1) * blk], wax_ref[n], preferred_element_type=F32)
        ga.append(res[:, :blk])
        gx.append(res[:, blk:])
    r = jax.nn.sigmoid(jnp.concatenate(ga, axis=1) + ba)
    ig = jax.nn.sigmoid(jnp.concatenate(gx, axis=1) + bx)

    nl = -lam
    softplus = jnp.maximum(nl, 0.0) + jnp.log1p(jnp.exp(-jnp.abs(nl)))
    log_a = -LRU_C * r * softplus
    a = jnp.exp(log_a)
    b = jnp.sqrt(-jnp.tanh(log_a) * (a * a + 1.0)) * (ig * u)

    a3 = a.reshape(groups, SUBLANES, w)
    b3 = b.reshape(groups, SUBLANES, w)
    row = lax.broadcasted_iota(jnp.int32, (groups, SUBLANES, w), 1)
    for s in (1, 2, 4):
        keep = row >= s
        b3 = jnp.where(keep, a3 * pltpu.roll(b3, s, 1) + b3, b3)
        a3 = jnp.where(keep, a3 * pltpu.roll(a3, s, 1), a3)
    hs = []
    for g in range(groups):
        hg = a3[g] * h_prev + b3[g]
        hs.append(hg)
        h_prev = hg[SUBLANES - 1:SUBLANES, :]
    h = jnp.concatenate(hs, axis=0)
    return h * _gelu_tanh(gb), h_prev


def _inproj_body(x_ref, sh_ref, sc_ref, w_ref, gq_ref, gk_ref, rc_ref, rs1_ref, rs2_ref,
                 h0_ref, c0_ref, cw_ref, cb_ref, wax_ref, ba_ref, bx_ref, lam_ref,
                 q_ref, k_ref, v_ref, y_ref, hn_ref, cn_ref,
                 xn_scr, ext_scr, gb_scr, h_scr, *, bb, tt, tn):
    i = pl.program_id(1)
    m = bb * tt
    d = x_ref.shape[-1]
    x = x_ref[...]
    ms = jnp.mean(x * x, axis=-1, keepdims=True)
    xn = x * lax.rsqrt(ms + EPS) * (1.0 + sc_ref[...]) + sh_ref[...]
    xn_scr[...] = xn.reshape(m, d).astype(BF16)

    @pl.when(i == 0)
    def _():
        ext_scr[:, 0:SUBLANES, :] = c0_ref[...]
        h_scr[...] = h0_ref[...]

    sec_w = q_ref.shape[-1]
    tiles = sec_w // tn

    def project(s, h):
        col = s * sec_w + h * tn
        return jnp.dot(xn_scr[...], w_ref[:, col:col + tn], preferred_element_type=F32)

    for h in range(tiles):
        ext_scr[:, SUBLANES:, h * tn:(h + 1) * tn] = project(3, h).reshape(bb, tt, tn)
    for h in range(tiles):
        gb_scr[:, h * tn:(h + 1) * tn] = project(4, h)
    for b in range(bb):
        y, h_last = _lru_tile(ext_scr.at[b], gb_scr[b * tt:(b + 1) * tt, :], h_scr[b],
                              cw_ref[...], cb_ref[...], wax_ref, ba_ref[...], bx_ref[...],
                              lam_ref[...])
        y_ref[b] = y.astype(BF16)
        h_scr[b] = h_last
    ext_scr[:, 0:SUBLANES, :] = ext_scr[:, tt:tt + SUBLANES, :]
    hn_ref[...] = h_scr[...]
    cn_ref[...] = ext_scr[:, 0:SUBLANES, :]

    tabs = (rc_ref[...], rs1_ref[...], rs2_ref[...])
    if bb > 1:
        tabs = tuple(jnp.concatenate([a] * bb, axis=0) for a in tabs)
    for h in range(tiles):
        acc = _qk_norm_rope(project(0, h), gq_ref[...], *tabs) * (QK_DIM ** -0.5)
        q_ref[:, :, h * tn:(h + 1) * tn] = acc.astype(BF16).reshape(bb, tt, tn)
    for h in range(tiles):
        acc = _qk_norm_rope(project(1, h), gk_ref[...], *tabs)
        k_ref[:, :, h * tn:(h + 1) * tn] = acc.reshape(bb, tt, tn)
    for h in range(tiles):
        v_ref[:, :, h * tn:(h + 1) * tn] = project(2, h).reshape(bb, tt, tn)


def _in_projection(x, mod, w_in, g_q, g_k, ropes, h0, conv0, lru_weights, bb, tt,
                   tn=PROJ_COL_TILE):
    b, t, d = x.shape
    n = w_in.shape[1]
    sec_w = n // 5
    rc, rs1, rs2 = ropes
    conv_w, conv_b, w_ax, b_a, b_x, lru_lambda = lru_weights
    w = conv_w.shape[-1]
    blk = w // N_LRU_BLOCKS
    c0 = jnp.pad(conv0, ((0, 0), (SUBLANES - (CONV_W - 1), 0), (0, 0)))
    sec_spec = pl.BlockSpec((bb, tt, sec_w), lambda bi, i: (bi, i, 0))
    tab_spec = pl.BlockSpec((tt, LANES), lambda bi, i: (i, 0))
    g_spec = pl.BlockSpec((1, LANES), lambda bi, i: (0, 0))
    state_spec = lambda rows: pl.BlockSpec((bb, rows, w), lambda bi, i: (bi, 0, 0))
    vec_spec = lambda rows: pl.BlockSpec((rows, w), lambda bi, i: (0, 0))
    f32_out = jax.ShapeDtypeStruct((b, t, sec_w), F32)
    q, k, v, y, hn, cn = pl.pallas_call(
        functools.partial(_inproj_body, bb=bb, tt=tt, tn=tn),
        grid=(b // bb, t // tt),
        in_specs=[pl.BlockSpec((bb, tt, d), lambda bi, i: (bi, i, 0)),
                  _mod_spec(bb, d, 0, 2), _mod_spec(bb, d, 1, 2),
                  pl.BlockSpec((d, n), lambda bi, i: (0, 0), pipeline_mode=pl.Buffered(1)),
                  g_spec, g_spec, tab_spec, tab_spec, tab_spec,
                  state_spec(1), state_spec(SUBLANES), vec_spec(CONV_W), vec_spec(1),
                  pl.BlockSpec((N_LRU_BLOCKS, blk, 2 * blk), lambda bi, i: (0, 0, 0)),
                  vec_spec(1), vec_spec(1), vec_spec(1)],
        out_specs=[sec_spec, sec_spec, sec_spec, sec_spec, state_spec(1), state_spec(SUBLANES)],
        out_shape=[jax.ShapeDtypeStruct((b, t, sec_w), BF16), f32_out, f32_out,
                   jax.ShapeDtypeStruct((b, t, w), BF16),
                   jax.ShapeDtypeStruct((b, 1, w), F32),
                   jax.ShapeDtypeStruct((b, SUBLANES, w), F32)],
        scratch_shapes=[pltpu.VMEM((bb * tt, d), BF16),
                        pltpu.VMEM((bb, tt + SUBLANES, w), F32),
                        pltpu.VMEM((bb * tt, w), F32),
                        pltpu.VMEM((bb, 1, w), F32)],
        compiler_params=_params(2),
        name="in_projection",
    )(x, mod, mod, w_in, g_q, g_k, rc, rs1, rs2, h0.reshape(b, 1, w), c0,
      conv_w, conv_b.reshape(1, w), w_ax, b_a.reshape(1, w), b_x.reshape(1, w),
      lru_lambda.reshape(1, w))
    return q, k, v, y, hn[:, 0], cn[:, SUBLANES - (CONV_W - 1):]


def _inproj_stream_body(x_ref, sh_ref, sc_ref, w_ref, gq_ref, gk_ref, rc_ref, rs1_ref, rs2_ref,
                        h0_ref, c0_ref, cw_ref, cb_ref, wax_ref, ba_ref, bx_ref, lam_ref,
                        q_ref, k_ref, v_ref, y_ref, hn_ref, cn_ref, wbf_ref,
                        xn_scr, ext_scr, gb_scr, *, bb, tt, tn):
    j = pl.program_id(0)
    m = bb * tt
    d = x_ref.shape[-1]

    @pl.when(j == 0)
    def _():
        x = x_ref[...]
        ms = jnp.mean(x * x, axis=-1, keepdims=True)
        xn = x * lax.rsqrt(ms + EPS) * (1.0 + sc_ref[...]) + sh_ref[...]
        xn_scr[...] = xn.reshape(m, d).astype(BF16)
        ext_scr[:, 0:SUBLANES, :] = c0_ref[...]

    wt = w_ref[...].astype(BF16)
    wbf_ref[...] = wt
    acc = jnp.dot(xn_scr[...], wt, preferred_element_type=F32)

    sec_w = q_ref.shape[-1]
    tiles = sec_w // tn

    def tables():
        tabs = (rc_ref[...], rs1_ref[...], rs2_ref[...])
        return tuple(jnp.concatenate([a] * bb, axis=0) for a in tabs) if bb > 1 else tabs

    for c in range(5 * tiles):
        s, h = divmod(c, tiles)
        cols = slice(h * tn, (h + 1) * tn)

        @pl.when(j == c)
        def _(s=s, h=h, cols=cols):
            if s == 0:
                y = _qk_norm_rope(acc, gq_ref[...], *tables()) * (QK_DIM ** -0.5)
                q_ref[:, :, cols] = y.astype(BF16).reshape(bb, tt, tn)
            elif s == 1:
                k_ref[:, :, cols] = _qk_norm_rope(acc, gk_ref[...], *tables()).reshape(bb, tt, tn)
            elif s == 2:
                v_ref[:, :, cols] = acc.reshape(bb, tt, tn)
            elif s == 3:
                ext_scr[:, SUBLANES:, cols] = acc.reshape(bb, tt, tn)
            else:
                gb_scr[:, cols] = acc
                if h == tiles - 1:
                    for b in range(bb):
                        y, h_last = _lru_tile(ext_scr.at[b], gb_scr[b * tt:(b + 1) * tt, :],
                                              h0_ref[b], cw_ref[...], cb_ref[...], wax_ref,
                                              ba_ref[...], bx_ref[...], lam_ref[...])
                        y_ref[b] = y.astype(BF16)
                        hn_ref[b] = h_last
                    cn_ref[...] = ext_scr[:, tt:tt + SUBLANES, :]


def _in_projection_streamed(x, mod, w_in_f32, g_q, g_k, ropes, h0, conv0, lru_weights,
                            tn=PROJ_COL_TILE):
    b, t, d = x.shape
    n = w_in_f32.shape[1]
    sec_w = n // 5
    rc, rs1, rs2 = ropes
    conv_w, conv_b, w_ax, b_a, b_x, lru_lambda = lru_weights
    w = conv_w.shape[-1]
    blk = w // N_LRU_BLOCKS
    c0 = jnp.pad(conv0, ((0, 0), (SUBLANES - (CONV_W - 1), 0), (0, 0)))
    whole = lambda *shape: pl.BlockSpec(shape, lambda j: (0,) * len(shape))
    mod_spec = lambda which: pl.BlockSpec((b, None, 1, d), lambda j: (0, which, 0, 0))
    col_tile = pl.BlockSpec((d, tn), lambda j: (0, j))
    f32_out = jax.ShapeDtypeStruct((b, t, sec_w), F32)
    q, k, v, y, hn, cn, w_bf = pl.pallas_call(
        functools.partial(_inproj_stream_body, bb=b, tt=t, tn=tn),
        grid=(n // tn,),
        in_specs=[whole(b, t, d), mod_spec(0), mod_spec(1), col_tile,
                  whole(1, LANES), whole(1, LANES),
                  whole(t, LANES), whole(t, LANES), whole(t, LANES),
                  whole(b, 1, w), whole(b, SUBLANES, w), whole(CONV_W, w), whole(1, w),
                  whole(N_LRU_BLOCKS, blk, 2 * blk), whole(1, w), whole(1, w), whole(1, w)],
        out_specs=[whole(b, t, sec_w), whole(b, t, sec_w), whole(b, t, sec_w), whole(b, t, w),
                   whole(b, 1, w), whole(b, SUBLANES, w), col_tile],
        out_shape=[jax.ShapeDtypeStruct((b, t, sec_w), BF16), f32_out, f32_out,
                   jax.ShapeDtypeStruct((b, t, w), BF16),
                   jax.ShapeDtypeStruct((b, 1, w), F32),
                   jax.ShapeDtypeStruct((b, SUBLANES, w), F32),
                   jax.ShapeDtypeStruct((d, n), BF16)],
        scratch_shapes=[pltpu.VMEM((b * t, d), BF16),
                        pltpu.VMEM((b, t + SUBLANES, w), F32),
                        pltpu.VMEM((b * t, w), F32)],
        compiler_params=_params(1),
        name="in_projection_streamed",
    )(x, mod, mod, w_in_f32, g_q, g_k, rc, rs1, rs2, h0.reshape(b, 1, w), c0,
      conv_w, conv_b.reshape(1, w), w_ax, b_a.reshape(1, w), b_x.reshape(1, w),
      lru_lambda.reshape(1, w))
    return (q, k, v, y, hn[:, 0], cn[:, SUBLANES - (CONV_W - 1):]), w_bf


def _lambda_value(lq1_ref, lk1_ref, lq2_ref, lk2_ref):
    s1 = jnp.sum(lq1_ref[...] * lk1_ref[...], axis=-1, keepdims=True)
    s2 = jnp.sum(lq2_ref[...] * lk2_ref[...], axis=-1, keepdims=True)
    return jnp.exp(s1) - jnp.exp(s2) + LAM_INIT


def _stack_components(q):
    lane = lax.broadcasted_iota(jnp.int32, q.shape, 1)
    zero = jnp.zeros_like(q)
    return jnp.concatenate([jnp.where(lane < QK_DIM, q, zero),
                            jnp.where(lane < QK_DIM, zero, q)], axis=0)


def _attn_prompt_body(lq1_ref, lk1_ref, lq2_ref, lk2_ref, g_ref, q_ref, k_ref, v_ref,
                      wo_ref, wu_ref, wd_ref, o_ref, wo_bf_ref, wu_bf_ref, wd_bf_ref,
                      kb_scr, vt_scr, s_scr, *, tq):
    for src, dst in ((wo_ref, wo_bf_ref), (wu_ref, wu_bf_ref), (wd_ref, wd_bf_ref)):
        dst[...] = src[...].astype(BF16)
    t = q_ref.shape[1]
    nq = t // tq
    kb_scr[...] = (k_ref[0] * LOG2E).astype(BF16)
    vt_scr[...] = v_ref[0].T.astype(BF16)
    lam = _lambda_value(lq1_ref, lk1_ref, lq2_ref, lk2_ref)
    gain = g_ref[...] * (1.0 - LAM_INIT)

    key_chunk = lax.broadcasted_iota(jnp.int32, (tq, 2 * tq), 0) // CHUNK
    qcol = lax.broadcasted_iota(jnp.int32, (tq, 2 * tq), 1)
    q_chunk = jnp.where(qcol >= tq, qcol - tq, qcol) // CHUNK
    visible = key_chunk <= q_chunk

    def fold(x):
        return x.reshape(tq // SUBLANES, SUBLANES, x.shape[-1])

    def stacked_q(i):
        return _stack_components(q_ref[0, i * tq:(i + 1) * tq, :])

    def score_chunk(i, c, qs, m8):
        sc = lax.dot_general(kb_scr[c * tq:(c + 1) * tq, :], qs, (((1,), (1,)), ((), ())),
                             preferred_element_type=F32)
        if c == i:
            sc = jnp.where(visible, sc, -1e30)
        s_scr[i % 2, c * tq:(c + 1) * tq, :] = sc
        return jnp.maximum(m8, jnp.max(fold(sc), axis=0))

    def value_chunk(i, c, m, l8, ot):
        p = jnp.exp2(s_scr[i % 2, c * tq:(c + 1) * tq, :] - m)
        ot = ot + jnp.dot(vt_scr[:, c * tq:(c + 1) * tq], p.astype(BF16),
                          preferred_element_type=F32)
        return l8 + jnp.sum(fold(p), axis=0), ot

    neg_inf = jnp.full((SUBLANES, 2 * tq), -jnp.inf, F32)
    qs = stacked_q(0)
    m8 = score_chunk(0, 0, qs, neg_inf)
    for i in range(nq):
        m = jnp.max(m8, axis=0, keepdims=True)
        l8 = jnp.zeros((SUBLANES, 2 * tq), F32)
        ot = jnp.zeros((V_DIM, 2 * tq), F32)
        if i + 1 < nq:
            qs, m8 = stacked_q(i + 1), neg_inf
        for c in range(i + 2):
            if c <= i:
                l8, ot = value_chunk(i, c, m, l8, ot)
            if i + 1 < nq:
                m8 = score_chunk(i + 1, c, qs, m8)
        ot = ot / jnp.sum(l8, axis=0, keepdims=True)
        o = ot[:, :tq] - lam * ot[:, tq:]
        ms = jnp.mean(o * o, axis=0, keepdims=True)
        o = o * lax.rsqrt(ms + EPS) * gain
        o_ref[0, i * tq:(i + 1) * tq, :] = o.T.astype(BF16)


def _attention_prompt(q, k, v, lams, g_subln, f32_weights, tq):
    b, t, aw = q.shape
    n_steps = b * N_HEADS
    head = pl.BlockSpec((1, t, V_DIM), lambda bi, h: (bi, 0, h))
    lam_spec = pl.BlockSpec((1, QK_DIM), lambda bi, h: (0, 0))

    def slice_spec(wt):
        rows = wt.shape[0] // n_steps
        assert rows * n_steps == wt.shape[0] and rows % (2 * SUBLANES) == 0
        return pl.BlockSpec((rows, wt.shape[1]), lambda bi, h: (bi * N_HEADS + h, 0))

    cast_specs = [slice_spec(wt) for wt in f32_weights]
    out, *casts = pl.pallas_call(
        functools.partial(_attn_prompt_body, tq=tq),
        grid=(b, N_HEADS),
        in_specs=[lam_spec] * 4 + [pl.BlockSpec((V_DIM, 1), lambda bi, h: (0, 0)),
                                   head, head, head] + cast_specs,
        out_specs=[head] + cast_specs,
        out_shape=[jax.ShapeDtypeStruct((b, t, aw), BF16)]
        + [jax.ShapeDtypeStruct(wt.shape, BF16) for wt in f32_weights],
        scratch_shapes=[pltpu.VMEM((t, V_DIM), BF16), pltpu.VMEM((V_DIM, t), BF16),
                        pltpu.VMEM((2, t, 2 * tq), F32)],
        compiler_params=_params(2),
        name="attention_prompt",
    )(*lams, g_subln.reshape(V_DIM, 1), q, k, v, *f32_weights)
    return out, casts


def _attn_sample_body(lq1_ref, lk1_ref, lq2_ref, lk2_ref, g_ref, q_ref, kt_ref, pv_ref,
                      nk_ref, nv_ref, carried_in_ref, o_ref, carried_out_ref, *, tq):
    lam = _lambda_value(lq1_ref, lk1_ref, lq2_ref, lk2_ref)
    gain = g_ref[...] * (1.0 - LAM_INIT)
    v_heads = jnp.swapaxes(pv_ref[...], 0, 1)
    for h in range(N_HEADS):
        cols = slice(h * V_DIM, (h + 1) * V_DIM)
        qs = _stack_components(q_ref[0, :, cols])
        s_past = jnp.dot(qs, kt_ref[0, cols, :].astype(BF16), preferred_element_type=F32)
        s_new = lax.dot_general(qs, nk_ref[0, :, cols].astype(BF16), (((1,), (1,)), ((), ())),
                                preferred_element_type=F32)
        m = jnp.maximum(jnp.max(s_past, axis=-1, keepdims=True),
                        jnp.max(s_new, axis=-1, keepdims=True))
        p_past = jnp.exp(s_past - m)
        p_new = jnp.exp(s_new - m)
        l = jnp.sum(p_past, axis=-1, keepdims=True) + jnp.sum(p_new, axis=-1, keepdims=True)
        o = jnp.dot(p_past.astype(BF16), v_heads[h].astype(BF16),
                    preferred_element_type=F32)
        o = o + jnp.dot(p_new.astype(BF16), nv_ref[0, :, cols].astype(BF16),
                        preferred_element_type=F32)
        o = o / l
        o = o[:tq] - lam * o[tq:]
        ms = jnp.mean(o * o, axis=-1, keepdims=True)
        o_ref[0, :, cols] = (o * lax.rsqrt(ms + EPS) * gain).astype(BF16)


def _attention_sample(q, past_kt, past_v, new_k, new_v, lams, g_subln, carried):
    b, t, aw = q.shape
    past = past_kt.shape[-1]
    assert past % CHUNK == 0 and t <= CHUNK
    row_new = pl.BlockSpec((1, t, aw), lambda bi: (bi, 0, 0))
    vec = lambda width: pl.BlockSpec((1, width), lambda bi: (0, 0))
    untouched = pl.BlockSpec(memory_space=pl.ANY)
    in_specs = [vec(QK_DIM)] * 4 + [
        vec(V_DIM), row_new,
        pl.BlockSpec((1, aw, past), lambda bi: (bi, 0, 0)),
        pl.BlockSpec((None, past, N_HEADS, V_DIM), lambda bi: (bi, 0, 0, 0)),
        row_new, row_new, untouched]
    return pl.pallas_call(
        functools.partial(_attn_sample_body, tq=t),
        grid=(b,),
        in_specs=in_specs,
        out_specs=[row_new, untouched],
        out_shape=[jax.ShapeDtypeStruct((b, t, aw), BF16),
                   jax.ShapeDtypeStruct(carried.shape, carried.dtype)],
        input_output_aliases={len(in_specs) - 1: 1},
        compiler_params=_params(1),
        name="attention_sample",
    )(*lams, g_subln, q, past_kt, past_v, new_k, new_v, carried)


def _mlp_body(x_ref, o_ref, y_ref, gt1_ref, sh2_ref, sc2_ref, gt2_ref, wo_ref, wu_ref, wd_ref,
              out_ref, xn_scr, acc_scr, *, bb, tt):
    f = pl.program_id(2)
    m = bb * tt
    d = x_ref.shape[-1]
    aw = o_ref.shape[-1]

    @pl.when(f == 0)
    def _():
        mix = jnp.dot(o_ref[...].reshape(m, aw), wo_ref[0:aw, :], preferred_element_type=F32)
        mix = mix + jnp.dot(y_ref[...].reshape(m, y_ref.shape[-1]), wo_ref[aw:, :],
                            preferred_element_type=F32)
        x1 = x_ref[...] + gt1_ref[...] * mix.reshape(bb, tt, d)
        out_ref[...] = x1
        ms = jnp.mean(x1 * x1, axis=-1, keepdims=True)
        xn = x1 * lax.rsqrt(ms + EPS) * (1.0 + sc2_ref[...]) + sh2_ref[...]
        xn_scr[...] = xn.reshape(m, d).astype(BF16)
        acc_scr[...] = jnp.zeros(acc_scr.shape, F32)

    hid = jnp.maximum(jnp.dot(xn_scr[...], wu_ref[...], preferred_element_type=F32), 0.0)
    acc_scr[...] += jnp.dot((hid * hid).astype(BF16), wd_ref[...], preferred_element_type=F32)

    @pl.when(f == pl.num_programs(2) - 1)
    def _():
        out_ref[...] = out_ref[...] + gt2_ref[...] * acc_scr[...].reshape(bb, tt, d)


def _outproj_mlp(x, o, y, mod, w_out, w_up, w_down, bb, tt, tf=MLP_HIDDEN_TILE):
    b, t, d = x.shape
    aw, lw = o.shape[-1], y.shape[-1]
    dff = w_up.shape[1]
    row = lambda width: pl.BlockSpec((bb, tt, width), lambda bi, i, f: (bi, i, 0))
    return pl.pallas_call(
        functools.partial(_mlp_body, bb=bb, tt=tt),
        grid=(b // bb, t // tt, dff // tf),
        in_specs=[row(d), row(aw), row(lw),
                  _mod_spec(bb, d, 2, 3), _mod_spec(bb, d, 3, 3),
                  _mod_spec(bb, d, 4, 3), _mod_spec(bb, d, 5, 3),
                  pl.BlockSpec((aw + lw, d), lambda bi, i, f: (0, 0),
                               pipeline_mode=pl.Buffered(1)),
                  pl.BlockSpec((d, tf), lambda bi, i, f: (0, f)),
                  pl.BlockSpec((tf, d), lambda bi, i, f: (f, 0))],
        out_specs=row(d),
        out_shape=jax.ShapeDtypeStruct((b, t, d), F32),
        scratch_shapes=[pltpu.VMEM((bb * tt, d), BF16), pltpu.VMEM((bb * tt, d), F32)],
        compiler_params=_params(3),
        name="outproj_mlp",
    )(x, o, y, mod, mod, mod, mod, w_out, w_up, w_down)


def kernel(x_prompt, x_sample, c_prompt, c_sample, cache_k, cache_v, state_lru_h, state_conv,
           w_ada, b_ada, w_in, g_q, g_k, lambda_q1, lambda_k1, lambda_q2, lambda_k2, g_subln,
           conv_w, conv_b, w_gate_a, b_gate_a, w_gate_x, b_gate_x, lru_lambda,
           w_out, w_up, w_down):
    bp, tp, d = x_prompt.shape
    bs, ts, _ = x_sample.shape
    past = cache_k.shape[2]
    aw = N_HEADS * V_DIM
    lw = conv_w.shape[-1]

    c_all = jnp.concatenate([c_prompt, c_sample], axis=0)
    mod = _modulation(c_all, w_ada[0], b_ada[0]).reshape(bp + bs, N_MOD, 1, d)
    mod_p, mod_s = mod[:bp], mod[bp:]

    rep = LANES // QK_DIM
    gq = jnp.tile(g_q[0], rep).reshape(1, LANES)
    gk = jnp.tile(g_k[0], rep).reshape(1, LANES)
    lru_weights = (conv_w[0], conv_b[0],
                   jnp.concatenate([w_gate_a[0], w_gate_x[0]], axis=-1).astype(BF16),
                   b_gate_a[0].reshape(-1), b_gate_x[0].reshape(-1), lru_lambda[0])
    lams = tuple(a[0].reshape(1, QK_DIM) for a in (lambda_q1, lambda_k1, lambda_q2, lambda_k2))
    g_sub = g_subln[0].reshape(1, V_DIM)

    (qs, ks, vs, ys, hs, cs), w_in_bf = _in_projection_streamed(
        x_sample, mod_s, w_in[0], gq, gk, _rope_tables(past + jnp.arange(ts)),
        state_lru_h[0], state_conv[0], lru_weights)
    past_kt = jnp.transpose(cache_k[0], (0, 2, 3, 4, 1)).reshape(bs, aw, past)
    osamp, w_in_bf = _attention_sample(qs, past_kt, cache_v[0], ks, vs, lams, g_sub, w_in_bf)

    tt_p = min(tp, ROW_TILE)
    qp, kp, vp, yp, hp, cp = _in_projection(
        x_prompt, mod_p, w_in_bf, gq, gk, _rope_tables(jnp.arange(tp)),
        jnp.zeros((bp, lw), F32), jnp.zeros((bp, CONV_W - 1, lw), F32), lru_weights, 1, tt_p)
    op, (wo, wu, wd) = _attention_prompt(qp, kp, vp, lams, g_sub,
                                         (w_out[0], w_up[0], w_down[0]), tq=min(tp, Q_BLOCK))
    out_p = _outproj_mlp(x_prompt, op, yp, mod_p, wo, wu, wd, 1, tt_p)
    out_s = _outproj_mlp(x_sample, osamp, ys, mod_s, wo, wu, wd, bs, ts)

    return (out_p, out_s,
            kp.reshape(1, bp, tp, N_HEADS, 2, QK_DIM), vp.reshape(1, bp, tp, N_HEADS, V_DIM),
            hp[None], cp[None],
            ks.reshape(1, bs, ts, N_HEADS, 2, QK_DIM), vs.reshape(1, bs, ts, N_HEADS, V_DIM),
            hs[None], cs[None])
```
